```python
import jax, jax.numpy as jnp
from jax import lax
import numpy as np

D_MODEL = 2048
BATCH = 4
SEQ = 4096
DEPTH = 1

GRID_W = 64
MEM_LEN = 256

ATT_HEAD_DIM = 128
ATT_WIDTH = D_MODEL // 2
ATT_HEADS = ATT_WIDTH // ATT_HEAD_DIM
ATT_KV_HEADS = 2
Q_BLOCK = 128
ROPE_THETA = 10000.0

ML_WIDTH = D_MODEL - ATT_WIDTH
ML_HEADS = 4
ML_V_DIM = ML_WIDTH // ML_HEADS
ML_QK_DIM = ML_V_DIM // 2
ML_CHUNK = 64

MIX_WIDTH = ATT_WIDTH + ML_WIDTH
IN_SPLITS = (ATT_WIDTH, ATT_KV_HEADS * ATT_HEAD_DIM, ATT_KV_HEADS * ATT_HEAD_DIM,
             ML_HEADS * ML_QK_DIM, ML_HEADS * ML_QK_DIM, ML_WIDTH, ML_WIDTH,
             2 * ML_HEADS, 2 * ML_HEADS)
IN_TOTAL = sum(IN_SPLITS)

XA_HEADS = 4
XA_HEAD_DIM = D_MODEL // XA_HEADS

PEER_HEADS = 8
PEER_NKEYS = 128
PEER_EXPERTS = PEER_NKEYS * PEER_NKEYS
PEER_TOPK = 16
PEER_QDIM = 256
PEER_BLOCK = 128

ALPHA = (2.0 * DEPTH) ** 0.25
BETA = (8.0 * DEPTH) ** -0.25

LN_EPS = 1e-5
RMS_EPS = 1e-6

kernel_name = "hybrid_attn_mlstm_peer_encoder"


def _layer_norm(x, g, b):
    xf = x.astype(jnp.float32)
    xc = xf - jnp.mean(xf, -1, keepdims=True)
    var = jnp.mean(xc * xc, -1, keepdims=True)
    y = xc * lax.rsqrt(var + LN_EPS) * g.astype(jnp.float32) + b.astype(jnp.float32)
    return y.astype(x.dtype)


def _rms_norm(x, g):
    xf = x.astype(jnp.float32)
    return xf * lax.rsqrt(jnp.mean(xf * xf, -1, keepdims=True) + RMS_EPS) * g.astype(jnp.float32)


def _axial_rope_tables(seq_len):
    rows = seq_len // GRID_W
    row = jnp.repeat(jnp.arange(rows, dtype=jnp.float32), GRID_W)
    col = jnp.tile(jnp.arange(GRID_W, dtype=jnp.float32), rows)
    n_freq = ATT_HEAD_DIM // 4
    inv_freq = ROPE_THETA ** (-jnp.arange(n_freq, dtype=jnp.float32) / n_freq)
    ang_r = row[:, None] * inv_freq
    ang_c = col[:, None] * inv_freq
    return (jnp.cos(ang_r), jnp.sin(ang_r), jnp.cos(ang_c), jnp.sin(ang_c))


def _rotate(z, cos, sin):
    z1, z2 = jnp.split(z, 2, axis=-1)
    c = cos[:, None, :]
    s = sin[:, None, :]
    return jnp.concatenate([z1 * c - z2 * s, z1 * s + z2 * c], axis=-1)


def _apply_axial_rope(z, tables):
    cos_r, sin_r, cos_c, sin_c = tables
    z_row, z_col = jnp.split(z, 2, axis=-1)
    return jnp.concatenate([_rotate(z_row, cos_r, sin_r), _rotate(z_col, cos_c, sin_c)], axis=-1)


def _blocked_gqa(q, k, v):
    B, S, H, Dh = q.shape
    G = H // ATT_KV_HEADS
    nb = S // Q_BLOCK
    qb = jnp.transpose(q.reshape(B, nb, Q_BLOCK, ATT_KV_HEADS, G, Dh), (1, 0, 2, 3, 4, 5))
    scale = Dh ** -0.5

    def block(q_blk):
        s = jnp.einsum('bqhgd,bkhd->bhgqk', q_blk, k, preferred_element_type=jnp.float32) * scale
        p = jax.nn.softmax(s, axis=-1)
        return jnp.einsum('bhgqk,bkhd->bqhgd', p.astype(v.dtype), v)

    o = lax.map(block, qb)
    return jnp.transpose(o, (1, 0, 2, 3, 4, 5)).reshape(B, S, H * Dh)


def _mlstm_chunkwise(q, k, v, log_f, log_i):
    B, H, S, dk = q.shape
    dv = v.shape[-1]
    L = ML_CHUNK
    nc = S // L

    def to_chunks(a):
        return jnp.moveaxis(a.reshape((B, H, nc, L) + a.shape[3:]), 2, 0)

    within = jnp.tril(jnp.ones((L, L), dtype=bool))

    def step(carry, inp):
        C, n, m = carry
        qc, kc, vc, fc, ic = inp
        b = jnp.cumsum(fc, axis=-1)
        log_d = b[..., :, None] - b[..., None, :] + ic[..., None, :]
        log_d = jnp.where(within, log_d, -jnp.inf)
        log_inter = b + m[..., None]
        m_row = jnp.maximum(log_inter, jnp.max(log_d, axis=-1))
        d = jnp.exp(log_d - m_row[..., None])
        w_inter = jnp.exp(log_inter - m_row)
        qk = jnp.einsum('bhld,bhsd->bhls', qc, kc) * d
        num = jnp.einsum('bhls,bhse->bhle', qk, vc) + w_inter[..., None] * jnp.einsum('bhld,bhde->bhle', qc, C)
        den = jnp.sum(qk, axis=-1) + w_inter * jnp.einsum('bhld,bhd->bhl', qc, n)
        h = num / jnp.maximum(jnp.abs(den), jnp.exp(-m_row))[..., None]
        b_last = b[..., -1]
        log_w = b_last[..., None] - b + ic
        m_new = jnp.maximum(b_last + m, jnp.max(log_w, axis=-1))
        w = jnp.exp(log_w - m_new[..., None])
        decay = jnp.exp(b_last + m - m_new)
        C = decay[..., None, None] * C + jnp.einsum('bhs,bhsd,bhse->bhde', w, kc, vc)
        n = decay[..., None] * n + jnp.einsum('bhs,bhsd->bhd', w, kc)
        return (C, n, m_new), h

    init = (jnp.zeros((B, H, dk, dv), jnp.float32), jnp.zeros((B, H, dk), jnp.float32),
            jnp.zeros((B, H), jnp.float32))
    _, h = lax.scan(step, init, (to_chunks(q), to_chunks(k), to_chunks(v), to_chunks(log_f), to_chunks(log_i)))
    return jnp.moveaxis(h, 0, 2).reshape(B, H, S, dv)


def _flip_seq(z):
    return jnp.flip(z, axis=2)


def _ml_heads(z, dim):
    B, S, _ = z.shape
    return jnp.transpose(z.reshape(B, S, ML_HEADS, dim).astype(jnp.float32), (0, 2, 1, 3))


def _hybrid_mixer(x, w_in, b_igate, b_fgate, att_q_norm, att_k_norm, ml_norm, w_out):
    B, S, _ = x.shape
    h = x @ w_in
    cuts = np.cumsum(IN_SPLITS)[:-1].tolist()
    aq, ak, av, mq, mk, mv, mo, gi, gf = jnp.split(h, cuts, axis=-1)

    tables = _axial_rope_tables(S)
    aq = _apply_axial_rope(_rms_norm(aq.reshape(B, S, ATT_HEADS, ATT_HEAD_DIM), att_q_norm), tables).astype(x.dtype)
    ak = _apply_axial_rope(_rms_norm(ak.reshape(B, S, ATT_KV_HEADS, ATT_HEAD_DIM), att_k_norm), tables).astype(x.dtype)
    av = av.reshape(B, S, ATT_KV_HEADS, ATT_HEAD_DIM)
    att = _blocked_gqa(aq, ak, av)

    mq = _ml_heads(mq, ML_QK_DIM)
    mk = _ml_heads(mk, ML_QK_DIM) * (ML_QK_DIM ** -0.5)
    mv = _ml_heads(mv, ML_V_DIM)
    log_i = jnp.transpose((gi.reshape(B, S, 2, ML_HEADS) + b_igate).astype(jnp.float32), (2, 0, 3, 1))
    log_f = jnp.transpose(jax.nn.log_sigmoid((gf.reshape(B, S, 2, ML_HEADS) + b_fgate).astype(jnp.float32)), (2, 0, 3, 1))
    h_fwd = _mlstm_chunkwise(mq, mk, mv, log_f[0], log_i[0])
    h_bwd = _flip_seq(_mlstm_chunkwise(_flip_seq(mq), _flip_seq(mk), _flip_seq(mv),
                                       _flip_seq(log_f[1]), _flip_seq(log_i[1])))
    hm = jnp.transpose(h_fwd + h_bwd, (0, 2, 1, 3))
    hm = _rms_norm(hm, ml_norm.reshape(ML_HEADS, ML_V_DIM)).reshape(B, S, ML_WIDTH)
    ml = (hm * jax.nn.sigmoid(mo.astype(jnp.float32))).astype(x.dtype)

    return jnp.concatenate([att, ml], axis=-1) @ w_out


def _memory_cross_attention(x, mem, wq, wk, wv, wo):
    B, S, D = x.shape
    M = mem.shape[1]
    q = (x @ wq).reshape(B, S, XA_HEADS, XA_HEAD_DIM)
    k = (mem @ wk).reshape(B, M, XA_HEADS, XA_HEAD_DIM)
    v = (mem @ wv).reshape(B, M, XA_HEADS, XA_HEAD_DIM)
    s = jnp.einsum('bqhd,bkhd->bhqk', q, k, preferred_element_type=jnp.float32) * (XA_HEAD_DIM ** -0.5)
    p = jax.nn.softmax(s, axis=-1)
    o = jnp.einsum('bhqk,bkhd->bqhd', p.astype(v.dtype), v).reshape(B, S, D)
    return o @ wo


def _peer(x, wq, subkeys, u, v):
    B, S, D = x.shape
    T = B * S
    xt = x.reshape(T, D)
    q = (xt @ wq).reshape(T, PEER_HEADS, 2, PEER_QDIM // 2)
    s = jnp.einsum('thcd,hckd->thck', q, subkeys).astype(jnp.float32)
    top_s, top_i = lax.top_k(s, PEER_TOPK)
    cand_s = top_s[:, :, 0, :, None] + top_s[:, :, 1, None, :]
    cand_e = top_i[:, :, 0, :, None] * PEER_NKEYS + top_i[:, :, 1, None, :]
    best_s, best_pos = lax.top_k(cand_s.reshape(T, PEER_HEADS, PEER_TOPK * PEER_TOPK), PEER_TOPK)
    experts = jnp.take_along_axis(cand_e.reshape(T, PEER_HEADS, PEER_TOPK * PEER_TOPK), best_pos, axis=-1)
    gates = jax.nn.softmax(best_s, axis=-1)
    E = PEER_HEADS * PEER_TOPK
    nb = T // PEER_BLOCK
    experts = experts.reshape(nb, PEER_BLOCK, E)
    gates = gates.reshape(nb, PEER_BLOCK, E).astype(x.dtype)

    def block(args):
        xb, eb, gb = args
        a = jax.nn.gelu(jnp.einsum('ted,td->te', u[eb], xb), approximate=False) * gb
        return jnp.einsum('te,ted->td', a, v[eb])

    out = lax.map(block, (xt.reshape(nb, PEER_BLOCK, D), experts, gates))
    return out.reshape(B, S, D)


def setup_inputs(seed: int = 0) -> dict:
    key = jax.random.key(seed)
    ks = jax.random.split(key, 23)
    f32 = jnp.float32
    D = D_MODEL

    def nrm(k, shape, scale):
        return jax.random.normal(k, shape, f32) * scale

    def gain(k, shape):
        return 1.0 + 0.02 * jax.random.normal(k, shape, f32)

    def bias(k, shape):
        return 0.02 * jax.random.normal(k, shape, f32)

    col_scale = jnp.asarray(np.concatenate(
        [np.full((n,), BETA if i in (2, 5) else 1.0, dtype=np.float32) for i, n in enumerate(IN_SPLITS)]))
    return {
        'x': nrm(ks[0], (BATCH, SEQ, D), 1.0),
        'mem': nrm(ks[1], (BATCH, MEM_LEN, D), 1.0),
        'w_in': nrm(ks[2], (DEPTH, D, IN_TOTAL), D ** -0.5) * col_scale,
        'b_igate': -1.0 + 0.1 * jax.random.normal(ks[3], (DEPTH, 2, ML_HEADS), f32),
        'b_fgate': jnp.linspace(3.0, 6.0, ML_HEADS, dtype=f32) + 0.1 * jax.random.normal(ks[4], (DEPTH, 2, ML_HEADS), f32),
        'att_q_norm': gain(ks[5], (DEPTH, ATT_HEAD_DIM)),
        'att_k_norm': gain(ks[6], (DEPTH, ATT_HEAD_DIM)),
        'ml_norm': gain(ks[7], (DEPTH, ML_WIDTH)),
        'w_out': nrm(ks[8], (DEPTH, MIX_WIDTH, D), MIX_WIDTH ** -0.5) * BETA,
        'ln1_g': gain(ks[9], (DEPTH, D)),
        'ln1_b': bias(ks[10], (DEPTH, D)),
        'xa_wq': nrm(ks[11], (DEPTH, D, D), D ** -0.5),
        'xa_wk': nrm(ks[12], (DEPTH, D, D), D ** -0.5),
        'xa_wv': nrm(ks[13], (DEPTH, D, D), D ** -0.5) * BETA,
        'xa_wo': nrm(ks[14], (DEPTH, D, D), D ** -0.5) * BETA,
        'ln2_g': gain(ks[15], (DEPTH, D)),
        'ln2_b': bias(ks[16], (DEPTH, D)),
        'peer_wq': nrm(ks[17], (DEPTH, D, PEER_HEADS * PEER_QDIM), D ** -0.5),
        'peer_subkeys': nrm(ks[18], (DEPTH, PEER_HEADS, 2, PEER_NKEYS, PEER_QDIM // 2), (PEER_QDIM // 2) ** -0.5),
        'peer_u': nrm(ks[19], (DEPTH, PEER_EXPERTS, D), D ** -0.5),
        'peer_v': nrm(ks[20], (DEPTH, PEER_EXPERTS, D), (PEER_HEADS * PEER_TOPK) ** -0.5) * BETA,
        'ln3_g': gain(ks[21], (DEPTH, D)),
        'ln3_b': bias(ks[22], (DEPTH, D)),
    }


def reference(x, mem, w_in, b_igate, b_fgate, att_q_norm, att_k_norm, ml_norm, w_out,
              ln1_g, ln1_b, xa_wq, xa_wk, xa_wv, xa_wo, ln2_g, ln2_b,
              peer_wq, peer_subkeys, peer_u, peer_v, ln3_g, ln3_b):
    for l in range(DEPTH):
        y = _hybrid_mixer(x, w_in[l], b_igate[l], b_fgate[l], att_q_norm[l], att_k_norm[l], ml_norm[l], w_out[l])
        x = _layer_norm(ALPHA * x + y, ln1_g[l], ln1_b[l])
        y = _memory_cross_attention(x, mem, xa_wq[l], xa_wk[l], xa_wv[l], xa_wo[l])
        x = _layer_norm(ALPHA * x + y, ln2_g[l], ln2_b[l])
        y = _peer(x, peer_wq[l], peer_subkeys[l], peer_u[l], peer_v[l])
        x = _layer_norm(ALPHA * x + y, ln3_g[l], ln3_b[l])
    return x
```

```python
import functools
import math

import jax
import jax.numpy as jnp
from jax import lax
from jax.experimental import pallas as pl
from jax.experimental.pallas import tpu as pltpu

F32 = jnp.float32
BF16 = jnp.bfloat16

GRID_W = 64
ATT_HEAD_DIM = 128
ATT_HEADS = 8
ATT_KV_HEADS = 2
ATT_WIDTH = ATT_HEADS * ATT_HEAD_DIM
ROPE_THETA = 10000.0
ML_HEADS = 4
ML_QK_DIM = 128
ML_V_DIM = 256
ML_WIDTH = ML_HEADS * ML_V_DIM
XA_HEADS = 4
PEER_HEADS = 8
PEER_NKEYS = 128
PEER_TOPK = 16
PEER_SLOTS = PEER_HEADS * PEER_TOPK
LN_EPS = 1e-5
RMS_EPS = 1e-6

OFF_AQ = 0
OFF_AK = OFF_AQ + ATT_WIDTH
OFF_AV = OFF_AK + ATT_KV_HEADS * ATT_HEAD_DIM
OFF_MQ = OFF_AV + ATT_KV_HEADS * ATT_HEAD_DIM
OFF_MK = OFF_MQ + ML_HEADS * ML_QK_DIM
OFF_MV = OFF_MK + ML_HEADS * ML_QK_DIM
OFF_MO = OFF_MV + ML_WIDTH
MAIN_COLS = OFF_MO + ML_WIDTH
N_GATES = 4 * ML_HEADS

LANES = 128
SUBLANES = 8
VMEM_LIMIT = 48 * 1024 * 1024

ML_CHUNK = 256
GATHER_TOKENS = 8


def _params(sem, vmem=VMEM_LIMIT):
    return pltpu.CompilerParams(dimension_semantics=sem, vmem_limit_bytes=vmem)


def _mm_kernel(a_ref, w_ref, o_ref, ab_ref, *, scale):
    @pl.when(pl.program_id(1) == 0)
    def _():
        ab_ref[...] = a_ref[...].astype(BF16)

    acc = jnp.dot(ab_ref[...], w_ref[...], preferred_element_type=F32)
    if scale != 1.0:
        acc = acc * scale
    o_ref[...] = acc.astype(o_ref.dtype)


def _matmul(a, w, *, bm, bn, out_dtype, name, scale=1.0):
    M, K = a.shape
    N = w.shape[1]
    assert M % bm == 0 and N % bn == 0
    return pl.pallas_call(
        functools.partial(_mm_kernel, scale=scale),
        out_shape=jax.ShapeDtypeStruct((M, N), out_dtype),
        grid=(M // bm, N // bn),
        in_specs=[pl.BlockSpec((bm, K), lambda i, j: (i, 0)),
                  pl.BlockSpec((K, bn), lambda i, j: (0, j))],
        out_specs=pl.BlockSpec((bm, bn), lambda i, j: (i, j)),
        scratch_shapes=[pltpu.VMEM((bm, K), BF16)],
        compiler_params=_params(("parallel", "arbitrary")),
        name=name,
    )(a, w)


def _qk_prep_kernel(h_ref, cos_ref, sin_ref, g_ref, o_ref):
    j = pl.program_id(1)
    z = h_ref[...].astype(F32)
    ms = jnp.mean(z * z, axis=-1, keepdims=True)
    zn = z * lax.rsqrt(ms + RMS_EPS) * g_ref[pl.ds(j, 1), :]
    lane = lax.broadcasted_iota(jnp.int32, zn.shape, 1)
    partner = jnp.where((lane % 64) < 32, pltpu.roll(zn, 96, 1), pltpu.roll(zn, 32, 1))
    o_ref[...] = (zn * cos_ref[...] + partner * sin_ref[...]).astype(o_ref.dtype)


def _qk_prep(hm, cos, sin, gains, *, S, tm=512):
    T = hm.shape[0]
    n_heads = ATT_HEADS + ATT_KV_HEADS
    s_blocks = S // tm
    return pl.pallas_call(
        _qk_prep_kernel,
        out_shape=jax.ShapeDtypeStruct((T, n_heads * ATT_HEAD_DIM), BF16),
        grid=(T // tm, n_heads),
        in_specs=[pl.BlockSpec((tm, LANES), lambda i, j: (i, j)),
                  pl.BlockSpec((tm, LANES), lambda i, j: (i % s_blocks, 0)),
                  pl.BlockSpec((tm, LANES), lambda i, j: (i % s_blocks, 0)),
                  pl.BlockSpec((16, LANES), lambda i, j: (0, 0))],
        out_specs=pl.BlockSpec((tm, LANES), lambda i, j: (i, j)),
        compiler_params=_params(("parallel", "arbitrary")),
        name="qk_prep",
    )(hm, cos, sin, gains)


def _attn_kernel(q_ref, k_ref, v_ref, o_ref):
    s = lax.dot_general(q_ref[...], k_ref[...], (((1,), (1,)), ((), ())),
                        preferred_element_type=F32)
    m = jnp.max(s, axis=-1, keepdims=True)
    p = jnp.exp(s - m)
    l = jnp.sum(p, axis=-1, keepdims=True)
    o = jnp.dot(p.astype(BF16), v_ref[...], preferred_element_type=F32)
    o_ref[...] = (o / l).astype(o_ref.dtype)


def _attention(qk3, hm3, *, tq=256):
    B, S, _ = qk3.shape
    group = ATT_HEADS // ATT_KV_HEADS
    k_blk0 = ATT_HEADS
    v_blk0 = OFF_AV // LANES
    return pl.pallas_call(
        _attn_kernel,
        out_shape=jax.ShapeDtypeStruct((B, S, ATT_WIDTH), BF16),
        grid=(B, ATT_HEADS, S // tq),
        in_specs=[pl.BlockSpec((None, tq, LANES), lambda b, h, i: (b, i, h)),
                  pl.BlockSpec((None, S, LANES), lambda b, h, i: (b, 0, k_blk0 + h // group)),
                  pl.BlockSpec((None, S, LANES), lambda b, h, i: (b, 0, v_blk0 + h // group))],
        out_specs=pl.BlockSpec((None, tq, LANES), lambda b, h, i: (b, i, h)),
        compiler_params=_params(("parallel", "parallel", "arbitrary")),
        name="gqa_attention",
    )(qk3, qk3, hm3)


def _log_sigmoid(x):
    return -(jnp.maximum(-x, 0.0) + jnp.log1p(jnp.exp(-jnp.abs(x))))


def _mlstm_kernel(bias_ref, q_ref, k_ref, v_ref, gi_row_ref, gf_row_ref, gi_col_ref, gf_col_ref,
                  o_ref, c_ref, n_ref, m_ref):
    h = pl.program_id(1)
    d = pl.program_id(2)
    c = pl.program_id(3)
    L = q_ref.shape[0]

    @pl.when(c == 0)
    def _():
        c_ref[...] = jnp.zeros_like(c_ref)
        n_ref[...] = jnp.zeros_like(n_ref)
        m_ref[...] = jnp.zeros_like(m_ref)

    b_i = bias_ref[d * ML_HEADS + h]
    b_f = bias_ref[2 * ML_HEADS + d * ML_HEADS + h]
    i_row = gi_row_ref[...] + b_i
    i_col = gi_col_ref[...] + b_i
    f_row = _log_sigmoid(gf_row_ref[...] + b_f)
    f_col = _log_sigmoid(gf_col_ref[...] + b_f)

    t_idx = lax.broadcasted_iota(jnp.int32, (L, L), 0)
    s_idx = lax.broadcasted_iota(jnp.int32, (L, L), 1)
    sgn = 1 - 2 * d
    within = (t_idx - s_idx) * sgn >= 0
    within_t = (s_idx - t_idx) * sgn >= 0
    b_col = jnp.sum(jnp.where(within, f_row, 0.0), axis=1, keepdims=True)
    b_row = jnp.sum(jnp.where(within_t, f_col, 0.0), axis=0, keepdims=True)
    b_tot = jnp.sum(f_row, axis=1, keepdims=True)

    m_prev = m_ref[...]
    log_d = jnp.where(within, b_col - b_row + i_row, -jnp.inf)
    log_inter = b_col + m_prev
    m_row = jnp.maximum(log_inter, jnp.max(log_d, axis=1, keepdims=True))
    dmat = jnp.exp(log_d - m_row)
    w_inter = jnp.exp(log_inter - m_row)

    q = q_ref[...]
    k = k_ref[...]
    v = v_ref[...]
    k_scale = ML_QK_DIM ** -0.5
    qk = lax.dot_general(q, k, (((1,), (1,)), ((), ())), preferred_element_type=F32)
    qk = qk * k_scale * dmat
    c_prev = c_ref[...]
    n_prev = n_ref[...]
    num = (jnp.dot(qk.astype(BF16), v, preferred_element_type=F32)
           + w_inter * jnp.dot(q, c_prev.astype(BF16), preferred_element_type=F32))
    den = (jnp.sum(qk, axis=1, keepdims=True)
           + w_inter * jnp.sum(q.astype(F32) * n_prev, axis=1, keepdims=True))
    o_ref[...] = (num / jnp.maximum(jnp.abs(den), jnp.exp(-m_row))).astype(o_ref.dtype)

    log_w = b_tot - b_col + i_col
    m_new = jnp.maximum(b_tot + m_prev, jnp.max(log_w, axis=0, keepdims=True))
    w = jnp.exp(log_w - m_new) * k_scale
    decay = jnp.exp(b_tot + m_prev - m_new)
    kw = k.astype(F32) * w
    c_ref[...] = decay * c_prev + lax.dot_general(
        kw.astype(BF16), v, (((0,), (0,)), ((), ())), preferred_element_type=F32)
    n_ref[...] = decay * n_prev + jnp.sum(kw, axis=0, keepdims=True)
    m_ref[...] = m_new


def _mlstm(hm3, gates_t, bias, *, L=ML_CHUNK):
    B, S, _ = hm3.shape
    nc = S // L
    g_row = gates_t.reshape(B, N_GATES, 1, S)
    g_col = gates_t.reshape(B, N_GATES, S, 1)

    def cidx(d, c):
        return c + d * (nc - 1 - 2 * c)

    q0, k0, v0 = OFF_MQ // ML_QK_DIM, OFF_MK // ML_QK_DIM, OFF_MV // ML_V_DIM
    return pl.pallas_call(
        _mlstm_kernel,
        out_shape=jax.ShapeDtypeStruct((2, B, S, ML_WIDTH), F32),
        grid=(B, ML_HEADS, 2, nc),
        in_specs=[
            pl.BlockSpec(memory_space=pltpu.SMEM),
            pl.BlockSpec((None, L, ML_QK_DIM), lambda b, h, d, c: (b, cidx(d, c), q0 + h)),
            pl.BlockSpec((None, L, ML_QK_DIM), lambda b, h, d, c: (b, cidx(d, c), k0 + h)),
            pl.BlockSpec((None, L, ML_V_DIM), lambda b, h, d, c: (b, cidx(d, c), v0 + h)),
            pl.BlockSpec((None, None, 1, L), lambda b, h, d, c: (b, d * ML_HEADS + h, 0, cidx(d, c))),
            pl.BlockSpec((None, None, 1, L),
                         lambda b, h, d, c: (b, 2 * ML_HEADS + d * ML_HEADS + h, 0, cidx(d, c))),
            pl.BlockSpec((None, None, L, 1), lambda b, h, d, c: (b, d * ML_HEADS + h, cidx(d, c), 0)),
            pl.BlockSpec((None, None, L, 1),
                         lambda b, h, d, c: (b, 2 * ML_HEADS + d * ML_HEADS + h, cidx(d, c), 0)),
        ],
        out_specs=pl.BlockSpec((None, None, L, ML_V_DIM), lambda b, h, d, c: (d, b, cidx(d, c), h)),
        scratch_shapes=[pltpu.VMEM((ML_QK_DIM, ML_V_DIM), F32),
                        pltpu.VMEM((1, ML_QK_DIM), F32),
                        pltpu.VMEM((1, 1), F32)],
        compiler_params=_params(("parallel", "parallel", "parallel", "arbitrary")),
        name="mlstm",
    )(bias, hm3, hm3, hm3, g_row, g_row, g_col, g_col)


def _ml_out_kernel(hf_ref, hb_ref, mo_ref, g_ref, o_ref):
    hsum = hf_ref[...] + hb_ref[...]
    ms = jnp.mean(hsum * hsum, axis=-1, keepdims=True)
    hn = hsum * lax.rsqrt(ms + RMS_EPS) * g_ref[...]
    o_ref[...] = (hn * jax.nn.sigmoid(mo_ref[...].astype(F32))).astype(o_ref.dtype)


def _ml_out(hdir, hm, ml_gain, *, tm=512):
    T = hm.shape[0]
    mo0 = OFF_MO // ML_V_DIM
    return pl.pallas_call(
        _ml_out_kernel,
        out_shape=jax.ShapeDtypeStruct((T, ML_WIDTH), BF16),
        grid=(T // tm, ML_HEADS),
        in_specs=[pl.BlockSpec((None, tm, ML_V_DIM), lambda i, h: (0, i, h)),
                  pl.BlockSpec((None, tm, ML_V_DIM), lambda i, h: (1, i, h)),
                  pl.BlockSpec((tm, ML_V_DIM), lambda i, h: (i, mo0 + h)),
                  pl.BlockSpec((None, 1, ML_V_DIM), lambda i, h: (h, 0, 0))],
        out_specs=pl.BlockSpec((tm, ML_V_DIM), lambda i, h: (i, h)),
        compiler_params=_params(("parallel", "arbitrary")),
        name="mlstm_out",
    )(hdir, hdir, hm, ml_gain)


def _layer_norm_rows(z, g, b):
    zc = z - jnp.mean(z, axis=-1, keepdims=True)
    var = jnp.mean(zc * zc, axis=-1, keepdims=True)
    return zc * lax.rsqrt(var + LN_EPS) * g + b


def _proj_ln_kernel(*refs, n_in, alpha):
    a_refs = refs[:n_in]
    w_refs = refs[n_in:2 * n_in]
    x_ref, g_ref, b_ref, o_ref = refs[2 * n_in:]
    acc = alpha * x_ref[...]
    for a_ref, w_ref in zip(a_refs, w_refs):
        acc = acc + jnp.dot(a_ref[...], w_ref[...], preferred_element_type=F32)
    o_ref[...] = _layer_norm_rows(acc, g_ref[...], b_ref[...])


def _proj_ln(a_list, w_list, x, g, b, *, alpha, name, bm=256):
    T, D = x.shape
    n_in = len(a_list)
    in_specs = ([pl.BlockSpec((bm, a.shape[1]), lambda i: (i, 0)) for a in a_list]
                + [pl.BlockSpec(w.shape, lambda i: (0, 0)) for w in w_list]
                + [pl.BlockSpec((bm, D), lambda i: (i, 0)),
                   pl.BlockSpec((1, D), lambda i: (0, 0)),
                   pl.BlockSpec((1, D), lambda i: (0, 0))])
    return pl.pallas_call(
        functools.partial(_proj_ln_kernel, n_in=n_in, alpha=alpha),
        out_shape=jax.ShapeDtypeStruct((T, D), F32),
        grid=(T // bm,),
        in_specs=in_specs,
        out_specs=pl.BlockSpec((bm, D), lambda i: (i, 0)),
        compiler_params=_params(("parallel",)),
        name=name,
    )(*a_list, *w_list, x, g.reshape(1, D), b.reshape(1, D))


def _res_ln_kernel(x_ref, y_ref, g_ref, b_ref, o_ref, *, alpha):
    o_ref[...] = _layer_norm_rows(alpha * x_ref[...] + y_ref[...], g_ref[...], b_ref[...])


def _res_ln(x, y, g, b, *, alpha, name, bm=512):
    T, D = x.shape
    return pl.pallas_call(
        functools.partial(_res_ln_kernel, alpha=alpha),
        out_shape=jax.ShapeDtypeStruct((T, D), F32),
        grid=(T // bm,),
        in_specs=[pl.BlockSpec((bm, D), lambda i: (i, 0)),
                  pl.BlockSpec((bm, D), lambda i: (i, 0)),
                  pl.BlockSpec((1, D), lambda i: (0, 0)),
                  pl.BlockSpec((1, D), lambda i: (0, 0))],
        out_specs=pl.BlockSpec((bm, D), lambda i: (i, 0)),
        compiler_params=_params(("parallel",)),
        name=name,
    )(x, y, g.reshape(1, D), b.reshape(1, D))


def _xattn_kernel(q_ref, k_ref, v_ref, o_ref):
    hd = q_ref.shape[1] // XA_HEADS
    for h in range(XA_HEADS):
        sl = slice(h * hd, (h + 1) * hd)
        s = lax.dot_general(q_ref[:, sl], k_ref[:, sl], (((1,), (1,)), ((), ())),
                            preferred_element_type=F32)
        m = jnp.max(s, axis=-1, keepdims=True)
        p = jnp.exp(s - m)
        l = jnp.sum(p, axis=-1, keepdims=True)
        o = jnp.dot(p.astype(BF16), v_ref[:, sl], preferred_element_type=F32)
        o_ref[:, sl] = (o / l).astype(o_ref.dtype)


def _xattn(q3, k3, v3, *, tq=512):
    B, S, D = q3.shape
    M = k3.shape[1]
    return pl.pallas_call(
        _xattn_kernel,
        out_shape=jax.ShapeDtypeStruct((B, S, D), BF16),
        grid=(B, S // tq),
        in_specs=[pl.BlockSpec((None, tq, D), lambda b, i: (b, i, 0)),
                  pl.BlockSpec((None, M, D), lambda b, i: (b, 0, 0)),
                  pl.BlockSpec((None, M, D), lambda b, i: (b, 0, 0))],
        out_specs=pl.BlockSpec((None, tq, D), lambda b, i: (b, i, 0)),
        compiler_params=_params(("parallel", "arbitrary")),
        name="mem_xattn",
    )(q3, k3, v3)


def _top_rows(s, payload, k):
    n = s.shape[0]
    iota = lax.broadcasted_iota(jnp.int32, s.shape, 0)
    vals, pays = [], []
    for _ in range(k):
        m = jnp.max(s, axis=0, keepdims=True)
        idx = jnp.min(jnp.where(s == m, iota, n), axis=0, keepdims=True)
        sel = iota == idx
        vals.append(m)
        pays.append(jnp.max(jnp.where(sel, payload, -1), axis=0, keepdims=True))
        s = jnp.where(sel, -jnp.inf, s)
    return jnp.concatenate(vals, axis=0), jnp.concatenate(pays, axis=0)


def _peer_route_kernel(x_ref, wq_ref, sk_ref, e_ref, g_ref, xb_ref):
    @pl.when(pl.program_id(1) == 0)
    def _():
        xb_ref[...] = x_ref[...].astype(BF16)

    q = jnp.dot(xb_ref[...], wq_ref[...], preferred_element_type=F32).astype(BF16)
    half = q.shape[1] // 2
    tops = []
    for c in range(2):
        s_t = lax.dot_general(sk_ref[0, c], q[:, c * half:(c + 1) * half],
                              (((1,), (1,)), ((), ())), preferred_element_type=F32)
        key_id = lax.broadcasted_iota(jnp.int32, s_t.shape, 0)
        tops.append(_top_rows(s_t, key_id, PEER_TOPK))
    (s0, i0), (s1, i1) = tops
    cand_s = jnp.concatenate([s0[a:a + 1, :] + s1 for a in range(PEER_TOPK)], axis=0)
    cand_e = jnp.concatenate([i0[a:a + 1, :] * PEER_NKEYS + i1 for a in range(PEER_TOPK)], axis=0)
    best_s, best_e = _top_rows(cand_s, cand_e, PEER_TOPK)
    p = jnp.exp(best_s - jnp.max(best_s, axis=0, keepdims=True))
    g_ref[...] = p / jnp.sum(p, axis=0, keepdims=True)
    e_ref[...] = best_e


def _peer_route(x, wq, subkeys, *, tm=128):
    T, D = x.shape
    qd = wq.shape[1] // PEER_HEADS
    return pl.pallas_call(
        _peer_route_kernel,
        out_shape=(jax.ShapeDtypeStruct((PEER_SLOTS, T), jnp.int32),
                   jax.ShapeDtypeStruct((PEER_SLOTS, T), F32)),
        grid=(T // tm, PEER_HEADS),
        in_specs=[pl.BlockSpec((tm, D), lambda i, h: (i, 0)),
                  pl.BlockSpec((D, qd), lambda i, h: (0, h)),
                  pl.BlockSpec((1, 2, PEER_NKEYS, qd // 2), lambda i, h: (h, 0, 0, 0))],
        out_specs=(pl.BlockSpec((PEER_TOPK, tm), lambda i, h: (h, i)),
                   pl.BlockSpec((PEER_TOPK, tm), lambda i, h: (h, i))),
        scratch_shapes=[pltpu.VMEM((tm, D), BF16)],
        compiler_params=_params(("parallel", "arbitrary")),
        name="peer_route",
    )(x, wq, subkeys)


def _gelu(x):
    return 0.5 * x * (1.0 + lax.erf(x * (1.0 / math.sqrt(2.0))))


def _peer_gather_kernel(e_first_ref, e_next_ref, x_ref, g_ref, uv_hbm, o_ref, buf, sem, *, n_steps):
    i = pl.program_id(0)
    tt = x_ref.shape[0]
    d = x_ref.shape[1]
    rows = tt * PEER_SLOTS

    def issue(e_ref, slot):
        def body(r, carry):
            e = e_ref[r // PEER_SLOTS, r % PEER_SLOTS]
            pltpu.make_async_copy(uv_hbm.at[pl.ds(e, 1)], buf.at[slot, pl.ds(r, 1)], sem.at[slot]).start()
            return carry
        lax.fori_loop(0, rows, body, 0, unroll=8)

    @pl.when(i == 0)
    def _():
        issue(e_first_ref, 0)

    @pl.when(i + 1 < n_steps)
    def _():
        issue(e_next_ref, (i + 1) % 2)

    slot = i % 2
    pltpu.make_async_copy(uv_hbm.at[pl.ds(0, rows)], buf.at[slot], sem.at[slot]).wait()

    xb = x_ref[...].astype(BF16)
    for t in range(tt):
        r0 = t * PEER_SLOTS
        u_t = buf[slot, r0:r0 + PEER_SLOTS, 0:d].astype(BF16)
        h_all = lax.dot_general(xb, u_t, (((1,), (1,)), ((), ())), preferred_element_type=F32)
        a = _gelu(h_all[t:t + 1, :]) * g_ref[t:t + 1, :]
        a8 = jnp.broadcast_to(a, (SUBLANES, PEER_SLOTS)).astype(BF16)
        v_t = buf[slot, r0:r0 + PEER_SLOTS, d:2 * d].astype(BF16)
        o = jnp.dot(a8, v_t, preferred_element_type=F32)
        o_ref[t:t + 1, :] = o[0:1, :]


def _peer_gather(x, experts, gates, uv, *, tt=GATHER_TOKENS):
    T, D = x.shape
    n_steps = T // tt
    return pl.pallas_call(
        functools.partial(_peer_gather_kernel, n_steps=n_steps),
        out_shape=jax.ShapeDtypeStruct((T, D), F32),
        grid=(n_steps,),
        in_specs=[pl.BlockSpec((tt, PEER_SLOTS), lambda i: (i, 0), memory_space=pltpu.SMEM),
                  pl.BlockSpec((tt, PEER_SLOTS), lambda i: (jnp.minimum(i + 1, n_steps - 1), 0),
                               memory_space=pltpu.SMEM),
                  pl.BlockSpec((tt, D), lambda i: (i, 0)),
                  pl.BlockSpec((tt, PEER_SLOTS), lambda i: (i, 0)),
                  pl.BlockSpec(memory_space=pl.ANY)],
        out_specs=pl.BlockSpec((tt, D), lambda i: (i, 0)),
        scratch_shapes=[pltpu.VMEM((2, tt * PEER_SLOTS, 2 * D), F32),
                        pltpu.SemaphoreType.DMA((2,))],
        compiler_params=_params(("arbitrary",)),
        name="peer_gather",
    )(experts, experts, x, gates, uv)


def _rope_tables(S):
    pos = jnp.arange(S, dtype=jnp.int32)
    row = (pos // GRID_W).astype(F32)
    col = (pos % GRID_W).astype(F32)
    n_freq = ATT_HEAD_DIM // 4
    inv_freq = ROPE_THETA ** (-jnp.arange(n_freq, dtype=F32) / n_freq)
    ang_r = row[:, None] * inv_freq
    ang_c = col[:, None] * inv_freq
    cos = jnp.concatenate([jnp.cos(ang_r), jnp.cos(ang_r), jnp.cos(ang_c), jnp.cos(ang_c)], axis=-1)
    sin = jnp.concatenate([-jnp.sin(ang_r), jnp.sin(ang_r), -jnp.sin(ang_c), jnp.sin(ang_c)], axis=-1)
    return cos, sin


def _layer(x, mem, w_in, b_igate, b_fgate, att_q_norm, att_k_norm, ml_norm, w_out, ln1_g, ln1_b,
           xa_wq, xa_wk, xa_wv, xa_wo, ln2_g, ln2_b, peer_wq, peer_subkeys, peer_u, peer_v,
           ln3_g, ln3_b, *, alpha):
    B, S, D = x.shape
    T = B * S
    M = mem.shape[1]
    xt = x.reshape(T, D)

    hm = _matmul(xt, w_in[:, :MAIN_COLS].astype(BF16), bm=512, bn=512, out_dtype=BF16, name="in_proj")
    w_gate = jnp.pad(w_in[:, MAIN_COLS:], ((0, 0), (0, LANES - N_GATES))).astype(BF16)
    gates = _matmul(xt, w_gate, bm=512, bn=LANES, out_dtype=F32, name="gate_proj")
    hm3 = hm.reshape(B, S, MAIN_COLS)

    cos, sin = _rope_tables(S)
    gains = jnp.concatenate([
        jnp.broadcast_to(att_q_norm * (ATT_HEAD_DIM ** -0.5), (ATT_HEADS, ATT_HEAD_DIM)),
        jnp.broadcast_to(att_k_norm, (ATT_KV_HEADS, ATT_HEAD_DIM)),
        jnp.zeros((16 - ATT_HEADS - ATT_KV_HEADS, ATT_HEAD_DIM), F32)], axis=0)
    qk = _qk_prep(hm, cos, sin, gains, S=S)
    att = _attention(qk.reshape(B, S, -1), hm3).reshape(T, ATT_WIDTH)

    gates_t = jnp.transpose(gates[:, :N_GATES].reshape(B, S, N_GATES), (0, 2, 1))
    bias = jnp.concatenate([b_igate.reshape(-1), b_fgate.reshape(-1)]).astype(F32)
    hdir = _mlstm(hm3, gates_t, bias).reshape(2, T, ML_WIDTH)
    ml = _ml_out(hdir, hm, ml_norm.reshape(ML_HEADS, 1, ML_V_DIM))

    w_out_b = w_out.astype(BF16)
    x1 = _proj_ln([att, ml], [w_out_b[:ATT_WIDTH], w_out_b[ATT_WIDTH:]], xt, ln1_g, ln1_b,
                  alpha=alpha, name="out_proj_ln1")

    hd = D // XA_HEADS
    xq = _matmul(x1, xa_wq.astype(BF16), bm=512, bn=512, out_dtype=BF16, name="xa_q", scale=hd ** -0.5)
    memt = mem.reshape(B * M, D)
    xk = _matmul(memt, xa_wk.astype(BF16), bm=256, bn=512, out_dtype=BF16, name="xa_k")
    xv = _matmul(memt, xa_wv.astype(BF16), bm=256, bn=512, out_dtype=BF16, name="xa_v")
    xo = _xattn(xq.reshape(B, S, D), xk.reshape(B, M, D), xv.reshape(B, M, D)).reshape(T, D)
    x2 = _proj_ln([xo], [xa_wo.astype(BF16)], x1, ln2_g, ln2_b, alpha=alpha, name="xa_out_ln2")

    e_t, g_t = _peer_route(x2, peer_wq.astype(BF16), peer_subkeys.astype(BF16))
    uv = jnp.concatenate([peer_u, peer_v], axis=1)
    y = _peer_gather(x2, e_t.T, g_t.T, uv)
    x3 = _res_ln(x2, y, ln3_g, ln3_b, alpha=alpha, name="peer_ln3")
    return x3.reshape(B, S, D)


def kernel(x, mem, w_in, b_igate, b_fgate, att_q_norm, att_k_norm, ml_norm, w_out, ln1_g, ln1_b,
           xa_wq, xa_wk, xa_wv, xa_wo, ln2_g, ln2_b, peer_wq, peer_subkeys, peer_u, peer_v,
           ln3_g, ln3_b):
    depth = w_in.shape[0]
    alpha = (2.0 * depth) ** 0.25
    for l in range(depth):
        x = _layer(x, mem, w_in[l], b_igate[l], b_fgate[l], att_q_norm[l], att_k_norm[l], ml_norm[l],
                   w_out[l], ln1_g[l], ln1_b[l], xa_wq[l], xa_wk[l], xa_wv[l], xa_wo[l], ln2_g[l],
                   ln2_b[l], peer_wq[l], peer_subkeys[l], peer_u[l], peer_v[l], ln3_g[l], ln3_b[l],
                   alpha=alpha)
    return x
```

```python
import functools
import math

import jax
import jax.numpy as jnp
from jax import lax
from jax.experimental import pallas as pl
from jax.experimental.pallas import tpu as pltpu

F32 = jnp.float32
BF16 = jnp.bfloat16

GRID_W = 64
ATT_HEAD_DIM = 128
ATT_HEADS = 8
ATT_KV_HEADS = 2
ATT_WIDTH = ATT_HEADS * ATT_HEAD_DIM
ROPE_THETA = 10000.0
ML_HEADS = 4
ML_QK_DIM = 128
ML_V_DIM = 256
ML_WIDTH = ML_HEADS * ML_V_DIM
XA_HEADS = 4
PEER_HEADS = 8
PEER_NKEYS = 128
PEER_TOPK = 16
PEER_SLOTS = PEER_HEADS * PEER_TOPK
LN_EPS = 1e-5
RMS_EPS = 1e-6

OFF_AQ = 0
OFF_AK = OFF_AQ + ATT_WIDTH
OFF_AV = OFF_AK + ATT_KV_HEADS * ATT_HEAD_DIM
OFF_MQ = OFF_AV + ATT_KV_HEADS * ATT_HEAD_DIM
OFF_MK = OFF_MQ + ML_HEADS * ML_QK_DIM
OFF_MV = OFF_MK + ML_HEADS * ML_QK_DIM
OFF_MO = OFF_MV + ML_WIDTH
MAIN_COLS = OFF_MO + ML_WIDTH
N_GATES = 4 * ML_HEADS

LANES = 128
SUBLANES = 8
VMEM_LIMIT = 48 * 1024 * 1024

ML_CHUNK = 256
GATHER_TOKENS = 8


def _params(sem, vmem=VMEM_LIMIT):
    return pltpu.CompilerParams(dimension_semantics=sem, vmem_limit_bytes=vmem)


def _mm_kernel(a_ref, w_ref, o_ref, ab_ref, *, scale):
    @pl.when(pl.program_id(1) == 0)
    def _():
        ab_ref[...] = a_ref[...].astype(BF16)

    acc = jnp.dot(ab_ref[...], w_ref[...], preferred_element_type=F32)
    if scale != 1.0:
        acc = acc * scale
    o_ref[...] = acc.astype(o_ref.dtype)


def _matmul(a, w, *, bm, bn, out_dtype, name, scale=1.0):
    M, K = a.shape
    N = w.shape[1]
    assert M % bm == 0 and N % bn == 0
    return pl.pallas_call(
        functools.partial(_mm_kernel, scale=scale),
        out_shape=jax.ShapeDtypeStruct((M, N), out_dtype),
        grid=(M // bm, N // bn),
        in_specs=[pl.BlockSpec((bm, K), lambda i, j: (i, 0)),
                  pl.BlockSpec((K, bn), lambda i, j: (0, j))],
        out_specs=pl.BlockSpec((bm, bn), lambda i, j: (i, j)),
        scratch_shapes=[pltpu.VMEM((bm, K), BF16)],
        compiler_params=_params(("parallel", "arbitrary")),
        name=name,
    )(a, w)


def _qk_prep_kernel(h_ref, cos_ref, sin_ref, g_ref, o_ref):
    j = pl.program_id(1)
    z = h_ref[...].astype(F32)
    ms = jnp.mean(z * z, axis=-1, keepdims=True)
    zn = z * lax.rsqrt(ms + RMS_EPS) * g_ref[pl.ds(j, 1), :]
    lane = lax.broadcasted_iota(jnp.int32, zn.shape, 1)
    partner = jnp.where((lane % 64) < 32, pltpu.roll(zn, 96, 1), pltpu.roll(zn, 32, 1))
    o_ref[...] = (zn * cos_ref[...] + partner * sin_ref[...]).astype(o_ref.dtype)


def _qk_prep(hm, cos, sin, gains, *, S, tm=512):
    T = hm.shape[0]
    n_heads = ATT_HEADS + ATT_KV_HEADS
    s_blocks = S // tm
    return pl.pallas_call(
        _qk_prep_kernel,
        out_shape=jax.ShapeDtypeStruct((T, n_heads * ATT_HEAD_DIM), BF16),
        grid=(T // tm, n_heads),
        in_specs=[pl.BlockSpec((tm, LANES), lambda i, j: (i, j)),
                  pl.BlockSpec((tm, LANES), lambda i, j: (i % s_blocks, 0)),
                  pl.BlockSpec((tm, LANES), lambda i, j: (i % s_blocks, 0)),
                  pl.BlockSpec((16, LANES), lambda i, j: (0, 0))],
        out_specs=pl.BlockSpec((tm, LANES), lambda i, j: (i, j)),
        compiler_params=_params(("parallel", "arbitrary")),
        name="qk_prep",
    )(hm, cos, sin, gains)


def _attn_kernel(q_ref, k_ref, v_ref, o_ref):
    s = lax.dot_general(q_ref[...], k_ref[...], (((1,), (1,)), ((), ())),
                        preferred_element_type=F32)
    m = jnp.max(s, axis=-1, keepdims=True)
    p = jnp.exp(s - m)
    l = jnp.sum(p, axis=-1, keepdims=True)
    o = jnp.dot(p.astype(BF16), v_ref[...], preferred_element_type=F32)
    o_ref[...] = (o / l).astype(o_ref.dtype)


def _attention(qk3, hm3, *, tq=256):
    B, S, _ = qk3.shape
    group = ATT_HEADS // ATT_KV_HEADS
    k_blk0 = ATT_HEADS
    v_blk0 = OFF_AV // LANES
    return pl.pallas_call(
        _attn_kernel,
        out_shape=jax.ShapeDtypeStruct((B, S, ATT_WIDTH), BF16),
        grid=(B, ATT_HEADS, S // tq),
        in_specs=[pl.BlockSpec((None, tq, LANES), lambda b, h, i: (b, i, h)),
                  pl.BlockSpec((None, S, LANES), lambda b, h, i: (b, 0, k_blk0 + h // group)),
                  pl.BlockSpec((None, S, LANES), lambda b, h, i: (b, 0, v_blk0 + h // group))],
        out_specs=pl.BlockSpec((None, tq, LANES), lambda b, h, i: (b, i, h)),
        compiler_params=_params(("parallel", "parallel", "arbitrary")),
        name="gqa_attention",
    )(qk3, qk3, hm3)


def _log_sigmoid(x):
    return -(jnp.maximum(-x, 0.0) + jnp.log1p(jnp.exp(-jnp.abs(x))))


def _mlstm_kernel(bias_ref, q_ref, k_ref, v_ref, gi_row_ref, gf_row_ref, gi_col_ref, gf_col_ref,
                  o_ref, c_ref, n_ref, m_ref):
    h = pl.program_id(1)
    d = pl.program_id(2)
    c = pl.program_id(3)
    L = q_ref.shape[0]

    @pl.when(c == 0)
    def _():
        c_ref[...] = jnp.zeros_like(c_ref)
        n_ref[...] = jnp.zeros_like(n_ref)
        m_ref[...] = jnp.zeros_like(m_ref)

    b_i = bias_ref[d * ML_HEADS + h]
    b_f = bias_ref[2 * ML_HEADS + d * ML_HEADS + h]
    i_row = gi_row_ref[...] + b_i
    i_col = gi_col_ref[...] + b_i
    f_row = _log_sigmoid(gf_row_ref[...] + b_f)
    f_col = _log_sigmoid(gf_col_ref[...] + b_f)

    t_idx = lax.broadcasted_iota(jnp.int32, (L, L), 0)
    s_idx = lax.broadcasted_iota(jnp.int32, (L, L), 1)
    sgn = 1 - 2 * d
    within = (t_idx - s_idx) * sgn >= 0
    within_t = (s_idx - t_idx) * sgn >= 0
    b_col = jnp.sum(jnp.where(within, f_row, 0.0), axis=1, keepdims=True)
    b_row = jnp.sum(jnp.where(within_t, f_col, 0.0), axis=0, keepdims=True)
    b_tot = jnp.sum(f_row, axis=1, keepdims=True)

    m_prev = m_ref[...]
    log_d = jnp.where(within, b_col - b_row + i_row, -jnp.inf)
    log_inter = b_col + m_prev
    m_row = jnp.maximum(log_inter, jnp.max(log_d, axis=1, keepdims=True))
    dmat = jnp.exp(log_d - m_row)
    w_inter = jnp.exp(log_inter - m_row)

    q = q_ref[...]
    k = k_ref[...]
    v = v_ref[...]
    k_scale = ML_QK_DIM ** -0.5
    qk = lax.dot_general(q, k, (((1,), (1,)), ((), ())), preferred_element_type=F32)
    qk = qk * k_scale * dmat
    c_prev = c_ref[...]
    n_prev = n_ref[...]
    num = (jnp.dot(qk.astype(BF16), v, preferred_element_type=F32)
           + w_inter * jnp.dot(q, c_prev.astype(BF16), preferred_element_type=F32))
    den = (jnp.sum(qk, axis=1, keepdims=True)
           + w_inter * jnp.sum(q.astype(F32) * n_prev, axis=1, keepdims=True))
    o_ref[...] = (num / jnp.maximum(jnp.abs(den), jnp.exp(-m_row))).astype(o_ref.dtype)

    log_w = b_tot - b_col + i_col
    m_new = jnp.maximum(b_tot + m_prev, jnp.max(log_w, axis=0, keepdims=True))
    w = jnp.exp(log_w - m_new) * k_scale
    decay = jnp.exp(b_tot + m_prev - m_new)
    kw = k.astype(F32) * w
    c_ref[...] = decay * c_prev + lax.dot_general(
        kw.astype(BF16), v, (((0,), (0,)), ((), ())), preferred_element_type=F32)
    n_ref[...] = decay * n_prev + jnp.sum(kw, axis=0, keepdims=True)
    m_ref[...] = m_new


def _mlstm(hm3, gates_t, bias, *, L=ML_CHUNK):
    B, S, _ = hm3.shape
    nc = S // L
    g_row = gates_t.reshape(B, N_GATES, 1, S)
    g_col = gates_t.reshape(B, N_GATES, S, 1)

    def cidx(d, c):
        return c + d * (nc - 1 - 2 * c)

    q0, k0, v0 = OFF_MQ // ML_QK_DIM, OFF_MK // ML_QK_DIM, OFF_MV // ML_V_DIM
    return pl.pallas_call(
        _mlstm_kernel,
        out_shape=jax.ShapeDtypeStruct((2, B, S, ML_WIDTH), F32),
        grid=(B, ML_HEADS, 2, nc),
        in_specs=[
            pl.BlockSpec(memory_space=pltpu.SMEM),
            pl.BlockSpec((None, L, ML_QK_DIM), lambda b, h, d, c: (b, cidx(d, c), q0 + h)),
            pl.BlockSpec((None, L, ML_QK_DIM), lambda b, h, d, c: (b, cidx(d, c), k0 + h)),
            pl.BlockSpec((None, L, ML_V_DIM), lambda b, h, d, c: (b, cidx(d, c), v0 + h)),
            pl.BlockSpec((None, None, 1, L), lambda b, h, d, c: (b, d * ML_HEADS + h, 0, cidx(d, c))),
            pl.BlockSpec((None, None, 1, L),
                         lambda b, h, d, c: (b, 2 * ML_HEADS + d * ML_HEADS + h, 0, cidx(d, c))),
            pl.BlockSpec((None, None, L, 1), lambda b, h, d, c: (b, d * ML_HEADS + h, cidx(d, c), 0)),
            pl.BlockSpec((None, None, L, 1),
                         lambda b, h, d, c: (b, 2 * ML_HEADS + d * ML_HEADS + h, cidx(d, c), 0)),
        ],
        out_specs=pl.BlockSpec((None, None, L, ML_V_DIM), lambda b, h, d, c: (d, b, cidx(d, c), h)),
        scratch_shapes=[pltpu.VMEM((ML_QK_DIM, ML_V_DIM), F32),
                        pltpu.VMEM((1, ML_QK_DIM), F32),
                        pltpu.VMEM((1, 1), F32)],
        compiler_params=_params(("parallel", "parallel", "parallel", "arbitrary")),
        name="mlstm",
    )(bias, hm3, hm3, hm3, g_row, g_row, g_col, g_col)


def _ml_out_kernel(hf_ref, hb_ref, mo_ref, g_ref, o_ref):
    hsum = hf_ref[...] + hb_ref[...]
    ms = jnp.mean(hsum * hsum, axis=-1, keepdims=True)
    hn = hsum * lax.rsqrt(ms + RMS_EPS) * g_ref[...]
    o_ref[...] = (hn * jax.nn.sigmoid(mo_ref[...].astype(F32))).astype(o_ref.dtype)


def _ml_out(hdir, hm, ml_gain, *, tm=512):
    T = hm.shape[0]
    mo0 = OFF_MO // ML_V_DIM
    return pl.pallas_call(
        _ml_out_kernel,
        out_shape=jax.ShapeDtypeStruct((T, ML_WIDTH), BF16),
        grid=(T // tm, ML_HEADS),
        in_specs=[pl.BlockSpec((None, tm, ML_V_DIM), lambda i, h: (0, i, h)),
                  pl.BlockSpec((None, tm, ML_V_DIM), lambda i, h: (1, i, h)),
                  pl.BlockSpec((tm, ML_V_DIM), lambda i, h: (i, mo0 + h)),
                  pl.BlockSpec((None, 1, ML_V_DIM), lambda i, h: (h, 0, 0))],
        out_specs=pl.BlockSpec((tm, ML_V_DIM), lambda i, h: (i, h)),
        compiler_params=_params(("parallel", "arbitrary")),
        name="mlstm_out",
    )(hdir, hdir, hm, ml_gain)


def _layer_norm_rows(z, g, b):
    zc = z - jnp.mean(z, axis=-1, keepdims=True)
    var = jnp.mean(zc * zc, axis=-1, keepdims=True)
    return zc * lax.rsqrt(var + LN_EPS) * g + b


def _proj_ln_kernel(*refs, n_in, alpha):
    a_refs = refs[:n_in]
    w_refs = refs[n_in:2 * n_in]
    x_ref, g_ref, b_ref, o_ref = refs[2 * n_in:]
    acc = alpha * x_ref[...]
    for a_ref, w_ref in zip(a_refs, w_refs):
        acc = acc + jnp.dot(a_ref[...], w_ref[...], preferred_element_type=F32)
    o_ref[...] = _layer_norm_rows(acc, g_ref[...], b_ref[...])


def _proj_ln(a_list, w_list, x, g, b, *, alpha, name, bm=256):
    T, D = x.shape
    n_in = len(a_list)
    in_specs = ([pl.BlockSpec((bm, a.shape[1]), lambda i: (i, 0)) for a in a_list]
                + [pl.BlockSpec(w.shape, lambda i: (0, 0)) for w in w_list]
                + [pl.BlockSpec((bm, D), lambda i: (i, 0)),
                   pl.BlockSpec((1, D), lambda i: (0, 0)),
                   pl.BlockSpec((1, D), lambda i: (0, 0))])
    return pl.pallas_call(
        functools.partial(_proj_ln_kernel, n_in=n_in, alpha=alpha),
        out_shape=jax.ShapeDtypeStruct((T, D), F32),
        grid=(T // bm,),
        in_specs=in_specs,
        out_specs=pl.BlockSpec((bm, D), lambda i: (i, 0)),
        compiler_params=_params(("parallel",)),
        name=name,
    )(*a_list, *w_list, x, g.reshape(1, D), b.reshape(1, D))


def _res_ln_kernel(x_ref, y_ref, g_ref, b_ref, o_ref, *, alpha):
    o_ref[...] = _layer_norm_rows(alpha * x_ref[...] + y_ref[...], g_ref[...], b_ref[...])


def _res_ln(x, y, g, b, *, alpha, name, bm=512):
    T, D = x.shape
    return pl.pallas_call(
        functools.partial(_res_ln_kernel, alpha=alpha),
        out_shape=jax.ShapeDtypeStruct((T, D), F32),
        grid=(T // bm,),
        in_specs=[pl.BlockSpec((bm, D), lambda i: (i, 0)),
                  pl.BlockSpec((bm, D), lambda i: (i, 0)),
                  pl.BlockSpec((1, D), lambda i: (0, 0)),
                  pl.BlockSpec((1, D), lambda i: (0, 0))],
        out_specs=pl.BlockSpec((bm, D), lambda i: (i, 0)),
        compiler_params=_params(("parallel",)),
        name=name,
    )(x, y, g.reshape(1, D), b.reshape(1, D))


def _xattn_kernel(q_ref, k_ref, v_ref, o_ref):
    hd = q_ref.shape[1] // XA_HEADS
    for h in range(XA_HEADS):
        sl = slice(h * hd, (h + 1) * hd)
        s = lax.dot_general(q_ref[:, sl], k_ref[:, sl], (((1,), (1,)), ((), ())),
                            preferred_element_type=F32)
        m = jnp.max(s, axis=-1, keepdims=True)
        p = jnp.exp(s - m)
        l = jnp.sum(p, axis=-1, keepdims=True)
        o = jnp.dot(p.astype(BF16), v_ref[:, sl], preferred_element_type=F32)
        o_ref[:, sl] = (o / l).astype(o_ref.dtype)


def _xattn(q3, k3, v3, *, tq=512):
    B, S, D = q3.shape
    M = k3.shape[1]
    return pl.pallas_call(
        _xattn_kernel,
        out_shape=jax.ShapeDtypeStruct((B, S, D), BF16),
        grid=(B, S // tq),
        in_specs=[pl.BlockSpec((None, tq, D), lambda b, i: (b, i, 0)),
                  pl.BlockSpec((None, M, D), lambda b, i: (b, 0, 0)),
                  pl.BlockSpec((None, M, D), lambda b, i: (b, 0, 0))],
        out_specs=pl.BlockSpec((None, tq, D), lambda b, i: (b, i, 0)),
        compiler_params=_params(("parallel", "arbitrary")),
        name="mem_xattn",
    )(q3, k3, v3)


def _top_rows(s, payload, k):
    n = s.shape[0]
    iota = lax.broadcasted_iota(jnp.int32, s.shape, 0)
    vals, pays = [], []
    for _ in range(k):
        m = jnp.max(s, axis=0, keepdims=True)
        idx = jnp.min(jnp.where(s == m, iota, n), axis=0, keepdims=True)
        sel = iota == idx
        vals.append(m)
        pays.append(jnp.max(jnp.where(sel, payload, -1), axis=0, keepdims=True))
        s = jnp.where(sel, -jnp.inf, s)
    return jnp.concatenate(vals, axis=0), jnp.concatenate(pays, axis=0)


def _peer_route_kernel(x_ref, wq_ref, sk_ref, e_ref, g_ref, xb_ref):
    @pl.when(pl.program_id(1) == 0)
    def _():
        xb_ref[...] = x_ref[...].astype(BF16)

    q = jnp.dot(xb_ref[...], wq_ref[...], preferred_element_type=F32).astype(BF16)
    half = q.shape[1] // 2
    tops = []
    for c in range(2):
        s_t = lax.dot_general(sk_ref[0, c], q[:, c * half:(c + 1) * half],
                              (((1,), (1,)), ((), ())), preferred_element_type=F32)
        key_id = lax.broadcasted_iota(jnp.int32, s_t.shape, 0)
        tops.append(_top_rows(s_t, key_id, PEER_TOPK))
    (s0, i0), (s1, i1) = tops
    cand_s = jnp.concatenate([s0[a:a + 1, :] + s1 for a in range(PEER_TOPK)], axis=0)
    cand_e = jnp.concatenate([i0[a:a + 1, :] * PEER_NKEYS + i1 for a in range(PEER_TOPK)], axis=0)
    best_s, best_e = _top_rows(cand_s, cand_e, PEER_TOPK)
    p = jnp.exp(best_s - jnp.max(best_s, axis=0, keepdims=True))
    g_ref[...] = p / jnp.sum(p, axis=0, keepdims=True)
    e_ref[...] = best_e


def _peer_route(x, wq, subkeys, *, tm=128):
    T, D = x.shape
    qd = wq.shape[1] // PEER_HEADS
    return pl.pallas_call(
        _peer_route_kernel,
        out_shape=(jax.ShapeDtypeStruct((PEER_SLOTS, T), jnp.int32),
                   jax.ShapeDtypeStruct((PEER_SLOTS, T), F32)),
        grid=(T // tm, PEER_HEADS),
        in_specs=[pl.BlockSpec((tm, D), lambda i, h: (i, 0)),
                  pl.BlockSpec((D, qd), lambda i, h: (0, h)),
                  pl.BlockSpec((1, 2, PEER_NKEYS, qd // 2), lambda i, h: (h, 0, 0, 0))],
        out_specs=(pl.BlockSpec((PEER_TOPK, tm), lambda i, h: (h, i)),
                   pl.BlockSpec((PEER_TOPK, tm), lambda i, h: (h, i))),
        scratch_shapes=[pltpu.VMEM((tm, D), BF16)],
        compiler_params=_params(("parallel", "arbitrary")),
        name="peer_route",
    )(x, wq, subkeys)


def _gelu(x):
    return 0.5 * x * (1.0 + lax.erf(x * (1.0 / math.sqrt(2.0))))


def _pack_expert_tables(u, v):
    half = u.shape[1] // 2

    def pack(a):
        b = lax.bitcast_convert_type(a.astype(BF16), jnp.uint16).astype(jnp.uint32)
        return b[:, :half] | (b[:, half:] << 16)

    return jnp.concatenate([pack(u), pack(v)], axis=1).reshape(u.shape[0], 1, 2 * half)


def _unpack_bf16_pair(w):
    lo = lax.bitcast_convert_type(w << 16, F32)
    hi = lax.bitcast_convert_type(w & jnp.uint32(0xFFFF0000), F32)
    return lo, hi


def _issue_rows(e_ref, t, table_hbm, dst, sem, lo, hi):
    for j in range(lo, hi):
        pltpu.make_async_copy(table_hbm.at[e_ref[t, j]], dst.at[pl.ds(t * PEER_SLOTS + j, 1)], sem).start()


def _wait_rows(table_hbm, dst, sem):
    pltpu.make_async_copy(table_hbm.at[pl.ds(0, dst.shape[0]), 0], dst, sem).wait()


def _peer_gather_step(e_next_ref, x_ref, g_ref, table_hbm, o_ref, cur, nxt, sem_cur, sem_nxt):
    tt, d = x_ref.shape
    half = d // 2
    n_chunks = half // LANES
    first = PEER_SLOTS // 2
    _wait_rows(table_hbm, cur, sem_cur)
    lane = lax.broadcasted_iota(jnp.int32, (PEER_SLOTS, tt), 1)
    hmat = jnp.zeros((PEER_SLOTS, tt), F32)
    for t in range(tt):
        _issue_rows(e_next_ref, t, table_hbm, nxt, sem_nxt, 0, first)
        r0 = t * PEER_SLOTS
        acc = None
        for c in range(n_chunks):
            lo, hi = _unpack_bf16_pair(cur[r0:r0 + PEER_SLOTS, c * LANES:(c + 1) * LANES])
            term = (lo * x_ref[t:t + 1, c * LANES:(c + 1) * LANES]
                    + hi * x_ref[t:t + 1, half + c * LANES:half + (c + 1) * LANES])
            acc = term if acc is None else acc + term
        hmat = jnp.where(lane == t, jnp.sum(acc, axis=1, keepdims=True), hmat)
    a = _gelu(hmat) * g_ref[...]
    for t in range(tt):
        _issue_rows(e_next_ref, t, table_hbm, nxt, sem_nxt, first, PEER_SLOTS)
        r0 = t * PEER_SLOTS
        a_t = a[:, t:t + 1]
        for c in range(n_chunks):
            lo, hi = _unpack_bf16_pair(cur[r0:r0 + PEER_SLOTS, half + c * LANES:half + (c + 1) * LANES])
            o_ref[t:t + 1, c * LANES:(c + 1) * LANES] = jnp.sum(lo * a_t, axis=0, keepdims=True)
            o_ref[t:t + 1, half + c * LANES:half + (c + 1) * LANES] = jnp.sum(hi * a_t, axis=0, keepdims=True)


def _peer_gather_kernel(e_first_ref, e_next_ref, x_ref, g_ref, table_hbm, o_ref, buf0, buf1, sem, *, n_steps):
    i = pl.program_id(0)
    tt = x_ref.shape[0]

    @pl.when(i == 0)
    def _():
        for t in range(tt):
            _issue_rows(e_first_ref, t, table_hbm, buf0, sem.at[0], 0, PEER_SLOTS)

    @pl.when(i % 2 == 0)
    def _():
        _peer_gather_step(e_next_ref, x_ref, g_ref, table_hbm, o_ref, buf0, buf1, sem.at[0], sem.at[1])

    @pl.when(i % 2 == 1)
    def _():
        _peer_gather_step(e_next_ref, x_ref, g_ref, table_hbm, o_ref, buf1, buf0, sem.at[1], sem.at[0])

    @pl.when(i == n_steps - 1)
    def _():
        @pl.when(i % 2 == 0)
        def _():
            _wait_rows(table_hbm, buf1, sem.at[1])

        @pl.when(i % 2 == 1)
        def _():
            _wait_rows(table_hbm, buf0, sem.at[0])


def _peer_gather(x, experts, gates_tiled, table, *, tt=GATHER_TOKENS):
    T, D = x.shape
    n_steps = T // tt
    return pl.pallas_call(
        functools.partial(_peer_gather_kernel, n_steps=n_steps),
        out_shape=jax.ShapeDtypeStruct((T, D), F32),
        grid=(n_steps,),
        in_specs=[pl.BlockSpec((tt, PEER_SLOTS), lambda i: (i, 0), memory_space=pltpu.SMEM),
                  pl.BlockSpec((tt, PEER_SLOTS), lambda i: (jnp.minimum(i + 1, n_steps - 1), 0),
                               memory_space=pltpu.SMEM),
                  pl.BlockSpec((tt, D), lambda i: (i, 0)),
                  pl.BlockSpec((None, PEER_SLOTS, tt), lambda i: (i, 0, 0)),
                  pl.BlockSpec(memory_space=pl.ANY)],
        out_specs=pl.BlockSpec((tt, D), lambda i: (i, 0)),
        scratch_shapes=[pltpu.VMEM((tt * PEER_SLOTS, D), jnp.uint32),
                        pltpu.VMEM((tt * PEER_SLOTS, D), jnp.uint32),
                        pltpu.SemaphoreType.DMA((2,))],
        compiler_params=_params(("arbitrary",)),
        name="peer_gather",
    )(experts, experts, x, gates_tiled, table)


def _rope_tables(S):
    pos = jnp.arange(S, dtype=jnp.int32)
    row = (pos // GRID_W).astype(F32)
    col = (pos % GRID_W).astype(F32)
    n_freq = ATT_HEAD_DIM // 4
    inv_freq = ROPE_THETA ** (-jnp.arange(n_freq, dtype=F32) / n_freq)
    ang_r = row[:, None] * inv_freq
    ang_c = col[:, None] * inv_freq
    cos = jnp.concatenate([jnp.cos(ang_r), jnp.cos(ang_r), jnp.cos(ang_c), jnp.cos(ang_c)], axis=-1)
    sin = jnp.concatenate([-jnp.sin(ang_r), jnp.sin(ang_r), -jnp.sin(ang_c), jnp.sin(ang_c)], axis=-1)
    return cos, sin


def _layer(x, mem, w_in, b_igate, b_fgate, att_q_norm, att_k_norm, ml_norm, w_out, ln1_g, ln1_b,
           xa_wq, xa_wk, xa_wv, xa_wo, ln2_g, ln2_b, peer_wq, peer_subkeys, peer_u, peer_v,
           ln3_g, ln3_b, *, alpha):
    B, S, D = x.shape
    T = B * S
    M = mem.shape[1]
    xt = x.reshape(T, D)

    hm = _matmul(xt, w_in[:, :MAIN_COLS].astype(BF16), bm=512, bn=512, out_dtype=BF16, name="in_proj")
    w_gate = jnp.pad(w_in[:, MAIN_COLS:], ((0, 0), (0, LANES - N_GATES))).astype(BF16)
    gates = _matmul(xt, w_gate, bm=512, bn=LANES, out_dtype=F32, name="gate_proj")
    hm3 = hm.reshape(B, S, MAIN_COLS)

    cos, sin = _rope_tables(S)
    gains = jnp.concatenate([
        jnp.broadcast_to(att_q_norm * (ATT_HEAD_DIM ** -0.5), (ATT_HEADS, ATT_HEAD_DIM)),
        jnp.broadcast_to(att_k_norm, (ATT_KV_HEADS, ATT_HEAD_DIM)),
        jnp.zeros((16 - ATT_HEADS - ATT_KV_HEADS, ATT_HEAD_DIM), F32)], axis=0)
    qk = _qk_prep(hm, cos, sin, gains, S=S)
    att = _attention(qk.reshape(B, S, -1), hm3).reshape(T, ATT_WIDTH)

    gates_t = jnp.transpose(gates[:, :N_GATES].reshape(B, S, N_GATES), (0, 2, 1))
    bias = jnp.concatenate([b_igate.reshape(-1), b_fgate.reshape(-1)]).astype(F32)
    hdir = _mlstm(hm3, gates_t, bias).reshape(2, T, ML_WIDTH)
    ml = _ml_out(hdir, hm, ml_norm.reshape(ML_HEADS, 1, ML_V_DIM))

    w_out_b = w_out.astype(BF16)
    x1 = _proj_ln([att, ml], [w_out_b[:ATT_WIDTH], w_out_b[ATT_WIDTH:]], xt, ln1_g, ln1_b,
                  alpha=alpha, name="out_proj_ln1")

    hd = D // XA_HEADS
    xq = _matmul(x1, xa_wq.astype(BF16), bm=512, bn=512, out_dtype=BF16, name="xa_q", scale=hd ** -0.5)
    memt = mem.reshape(B * M, D)
    xk = _matmul(memt, xa_wk.astype(BF16), bm=256, bn=512, out_dtype=BF16, name="xa_k")
    xv = _matmul(memt, xa_wv.astype(BF16), bm=256, bn=512, out_dtype=BF16, name="xa_v")
    xo = _xattn(xq.reshape(B, S, D), xk.reshape(B, M, D), xv.reshape(B, M, D)).reshape(T, D)
    x2 = _proj_ln([xo], [xa_wo.astype(BF16)], x1, ln2_g, ln2_b, alpha=alpha, name="xa_out_ln2")

    e_t, g_t = _peer_route(x2, peer_wq.astype(BF16), peer_subkeys.astype(BF16))
    gates_tiled = jnp.transpose(g_t.reshape(PEER_SLOTS, T // GATHER_TOKENS, GATHER_TOKENS), (1, 0, 2))
    y = _peer_gather(x2, e_t.T, gates_tiled, _pack_expert_tables(peer_u, peer_v))
    x3 = _res_ln(x2, y, ln3_g, ln3_b, alpha=alpha, name="peer_ln3")
    return x3.reshape(B, S, D)


def kernel(x, mem, w_in, b_igate, b_fgate, att_q_norm, att_k_norm, ml_norm, w_out, ln1_g, ln1_b,
           xa_wq, xa_wk, xa_wv, xa_wo, ln2_g, ln2_b, peer_wq, peer_subkeys, peer_u, peer_v,
           ln3_g, ln3_b):
    depth = w_in.shape[0]
    alpha = (2.0 * depth) ** 0.25
    for l in range(depth):
        x = _layer(x, mem, w_in[l], b_igate[l], b_fgate[l], att_q_norm[l], att_k_norm[l], ml_norm[l],
                   w_out[l], ln1_g[l], ln1_b[l], xa_wq[l], xa_wk[l], xa_wv[l], xa_wo[l], ln2_g[l],
                   ln2_b[l], peer_wq[l], peer_subkeys[l], peer_u[l], peer_v[l], ln3_g[l], ln3_b[l],
                   alpha=alpha)
    return x
```

```python
import functools
import math

import jax
import jax.numpy as jnp
from jax import lax
from jax.experimental import pallas as pl
from jax.experimental.pallas import tpu as pltpu

F32 = jnp.float32
BF16 = jnp.bfloat16

GRID_W = 64
ATT_HEAD_DIM = 128
ATT_HEADS = 8
ATT_KV_HEADS = 2
ATT_WIDTH = ATT_HEADS * ATT_HEAD_DIM
ROPE_THETA = 10000.0
ML_HEADS = 4
ML_QK_DIM = 128
ML_V_DIM = 256
ML_WIDTH = ML_HEADS * ML_V_DIM
XA_HEADS = 4
PEER_HEADS = 8
PEER_NKEYS = 128
PEER_TOPK = 16
PEER_SLOTS = PEER_HEADS * PEER_TOPK
LN_EPS = 1e-5
RMS_EPS = 1e-6

OFF_AQ = 0
OFF_AK = OFF_AQ + ATT_WIDTH
OFF_AV = OFF_AK + ATT_KV_HEADS * ATT_HEAD_DIM
OFF_MQ = OFF_AV + ATT_KV_HEADS * ATT_HEAD_DIM
OFF_MK = OFF_MQ + ML_HEADS * ML_QK_DIM
OFF_MV = OFF_MK + ML_HEADS * ML_QK_DIM
OFF_MO = OFF_MV + ML_WIDTH
MAIN_COLS = OFF_MO + ML_WIDTH
N_GATES = 4 * ML_HEADS

LANES = 128
SUBLANES = 8
VMEM_LIMIT = 48 * 1024 * 1024

ML_CHUNK = 256
GATHER_TOKENS = 8


def _params(sem, vmem=VMEM_LIMIT):
    return pltpu.CompilerParams(dimension_semantics=sem, vmem_limit_bytes=vmem)


def _mm_kernel(a_ref, w_ref, o_ref, ab_ref, *, scale):
    @pl.when(pl.program_id(1) == 0)
    def _():
        ab_ref[...] = a_ref[...].astype(BF16)

    acc = jnp.dot(ab_ref[...], w_ref[...], preferred_element_type=F32)
    if scale != 1.0:
        acc = acc * scale
    o_ref[...] = acc.astype(o_ref.dtype)


def _matmul(a, w, *, bm, bn, out_dtype, name, scale=1.0):
    M, K = a.shape
    N = w.shape[1]
    assert M % bm == 0 and N % bn == 0
    return pl.pallas_call(
        functools.partial(_mm_kernel, scale=scale),
        out_shape=jax.ShapeDtypeStruct((M, N), out_dtype),
        grid=(M // bm, N // bn),
        in_specs=[pl.BlockSpec((bm, K), lambda i, j: (i, 0)),
                  pl.BlockSpec((K, bn), lambda i, j: (0, j))],
        out_specs=pl.BlockSpec((bm, bn), lambda i, j: (i, j)),
        scratch_shapes=[pltpu.VMEM((bm, K), BF16)],
        compiler_params=_params(("parallel", "arbitrary")),
        name=name,
    )(a, w)


def _qk_prep_kernel(h_ref, cos_ref, sin_ref, g_ref, o_ref):
    j = pl.program_id(1)
    z = h_ref[...].astype(F32)
    ms = jnp.mean(z * z, axis=-1, keepdims=True)
    zn = z * lax.rsqrt(ms + RMS_EPS) * g_ref[pl.ds(j, 1), :]
    lane = lax.broadcasted_iota(jnp.int32, zn.shape, 1)
    partner = jnp.where((lane % 64) < 32, pltpu.roll(zn, 96, 1), pltpu.roll(zn, 32, 1))
    o_ref[...] = (zn * cos_ref[...] + partner * sin_ref[...]).astype(o_ref.dtype)


def _qk_prep(hm, cos, sin, gains, *, S, tm=512):
    T = hm.shape[0]
    n_heads = ATT_HEADS + ATT_KV_HEADS
    s_blocks = S // tm
    return pl.pallas_call(
        _qk_prep_kernel,
        out_shape=jax.ShapeDtypeStruct((T, n_heads * ATT_HEAD_DIM), BF16),
        grid=(T // tm, n_heads),
        in_specs=[pl.BlockSpec((tm, LANES), lambda i, j: (i, j)),
                  pl.BlockSpec((tm, LANES), lambda i, j: (i % s_blocks, 0)),
                  pl.BlockSpec((tm, LANES), lambda i, j: (i % s_blocks, 0)),
                  pl.BlockSpec((16, LANES), lambda i, j: (0, 0))],
        out_specs=pl.BlockSpec((tm, LANES), lambda i, j: (i, j)),
        compiler_params=_params(("parallel", "arbitrary")),
        name="qk_prep",
    )(hm, cos, sin, gains)


def _attn_kernel(q_ref, k_ref, v_ref, o_ref):
    s = lax.dot_general(q_ref[...], k_ref[...], (((1,), (1,)), ((), ())),
                        preferred_element_type=F32)
    m = jnp.max(s, axis=-1, keepdims=True)
    p = jnp.exp(s - m)
    l = jnp.sum(p, axis=-1, keepdims=True)
    o = jnp.dot(p.astype(BF16), v_ref[...], preferred_element_type=F32)
    o_ref[...] = (o / l).astype(o_ref.dtype)


def _attention(qk3, hm3, *, tq=256):
    B, S, _ = qk3.shape
    group = ATT_HEADS // ATT_KV_HEADS
    k_blk0 = ATT_HEADS
    v_blk0 = OFF_AV // LANES
    return pl.pallas_call(
        _attn_kernel,
        out_shape=jax.ShapeDtypeStruct((B, S, ATT_WIDTH), BF16),
        grid=(B, ATT_HEADS, S // tq),
        in_specs=[pl.BlockSpec((None, tq, LANES), lambda b, h, i: (b, i, h)),
                  pl.BlockSpec((None, S, LANES), lambda b, h, i: (b, 0, k_blk0 + h // group)),
                  pl.BlockSpec((None, S, LANES), lambda b, h, i: (b, 0, v_blk0 + h // group))],
        out_specs=pl.BlockSpec((None, tq, LANES), lambda b, h, i: (b, i, h)),
        compiler_params=_params(("parallel", "parallel", "arbitrary")),
        name="gqa_attention",
    )(qk3, qk3, hm3)


def _log_sigmoid(x):
    return -(jnp.maximum(-x, 0.0) + jnp.log1p(jnp.exp(-jnp.abs(x))))


def _mlstm_kernel(bias_ref, q_ref, k_ref, v_ref, gi_row_ref, gf_row_ref, gi_col_ref, gf_col_ref,
                  o_ref, c_ref, n_ref, m_ref):
    h = pl.program_id(1)
    d = pl.program_id(2)
    c = pl.program_id(3)
    L = q_ref.shape[0]

    @pl.when(c == 0)
    def _():
        c_ref[...] = jnp.zeros_like(c_ref)
        n_ref[...] = jnp.zeros_like(n_ref)
        m_ref[...] = jnp.zeros_like(m_ref)

    b_i = bias_ref[d * ML_HEADS + h]
    b_f = bias_ref[2 * ML_HEADS + d * ML_HEADS + h]
    i_row = gi_row_ref[...] + b_i
    i_col = gi_col_ref[...] + b_i
    f_row = _log_sigmoid(gf_row_ref[...] + b_f)
    f_col = _log_sigmoid(gf_col_ref[...] + b_f)

    t_idx = lax.broadcasted_iota(jnp.int32, (L, L), 0)
    s_idx = lax.broadcasted_iota(jnp.int32, (L, L), 1)
    sgn = 1 - 2 * d
    within = (t_idx - s_idx) * sgn >= 0
    within_t = (s_idx - t_idx) * sgn >= 0
    b_col = jnp.sum(jnp.where(within, f_row, 0.0), axis=1, keepdims=True)
    b_row = jnp.sum(jnp.where(within_t, f_col, 0.0), axis=0, keepdims=True)
    b_tot = jnp.sum(f_row, axis=1, keepdims=True)

    m_prev = m_ref[...]
    log_d = jnp.where(within, b_col - b_row + i_row, -jnp.inf)
    log_inter = b_col + m_prev
    m_row = jnp.maximum(log_inter, jnp.max(log_d, axis=1, keepdims=True))
    dmat = jnp.exp(log_d - m_row)
    w_inter = jnp.exp(log_inter - m_row)

    q = q_ref[...]
    k = k_ref[...]
    v = v_ref[...]
    k_scale = ML_QK_DIM ** -0.5
    qk = lax.dot_general(q, k, (((1,), (1,)), ((), ())), preferred_element_type=F32)
    qk = qk * k_scale * dmat
    c_prev = c_ref[...]
    n_prev = n_ref[...]
    num = (jnp.dot(qk.astype(BF16), v, preferred_element_type=F32)
           + w_inter * jnp.dot(q, c_prev.astype(BF16), preferred_element_type=F32))
    den = (jnp.sum(qk, axis=1, keepdims=True)
           + w_inter * jnp.sum(q.astype(F32) * n_prev, axis=1, keepdims=True))
    o_ref[...] = (num / jnp.maximum(jnp.abs(den), jnp.exp(-m_row))).astype(o_ref.dtype)

    log_w = b_tot - b_col + i_col
    m_new = jnp.maximum(b_tot + m_prev, jnp.max(log_w, axis=0, keepdims=True))
    w = jnp.exp(log_w - m_new) * k_scale
    decay = jnp.exp(b_tot + m_prev - m_new)
    kw = k.astype(F32) * w
    c_ref[...] = decay * c_prev + lax.dot_general(
        kw.astype(BF16), v, (((0,), (0,)), ((), ())), preferred_element_type=F32)
    n_ref[...] = decay * n_prev + jnp.sum(kw, axis=0, keepdims=True)
    m_ref[...] = m_new


def _mlstm(hm3, gates_t, bias, *, L=ML_CHUNK):
    B, S, _ = hm3.shape
    nc = S // L
    g_row = gates_t.reshape(B, N_GATES, 1, S)
    g_col = gates_t.reshape(B, N_GATES, S, 1)

    def cidx(d, c):
        return c + d * (nc - 1 - 2 * c)

    q0, k0, v0 = OFF_MQ // ML_QK_DIM, OFF_MK // ML_QK_DIM, OFF_MV // ML_V_DIM
    return pl.pallas_call(
        _mlstm_kernel,
        out_shape=jax.ShapeDtypeStruct((2, B, S, ML_WIDTH), F32),
        grid=(B, ML_HEADS, 2, nc),
        in_specs=[
            pl.BlockSpec(memory_space=pltpu.SMEM),
            pl.BlockSpec((None, L, ML_QK_DIM), lambda b, h, d, c: (b, cidx(d, c), q0 + h)),
            pl.BlockSpec((None, L, ML_QK_DIM), lambda b, h, d, c: (b, cidx(d, c), k0 + h)),
            pl.BlockSpec((None, L, ML_V_DIM), lambda b, h, d, c: (b, cidx(d, c), v0 + h)),
            pl.BlockSpec((None, None, 1, L), lambda b, h, d, c: (b, d * ML_HEADS + h, 0, cidx(d, c))),
            pl.BlockSpec((None, None, 1, L),
                         lambda b, h, d, c: (b, 2 * ML_HEADS + d * ML_HEADS + h, 0, cidx(d, c))),
            pl.BlockSpec((None, None, L, 1), lambda b, h, d, c: (b, d * ML_HEADS + h, cidx(d, c), 0)),
            pl.BlockSpec((None, None, L, 1),
                         lambda b, h, d, c: (b, 2 * ML_HEADS + d * ML_HEADS + h, cidx(d, c), 0)),
        ],
        out_specs=pl.BlockSpec((None, None, L, ML_V_DIM), lambda b, h, d, c: (d, b, cidx(d, c), h)),
        scratch_shapes=[pltpu.VMEM((ML_QK_DIM, ML_V_DIM), F32),
                        pltpu.VMEM((1, ML_QK_DIM), F32),
                        pltpu.VMEM((1, 1), F32)],
        compiler_params=_params(("parallel", "parallel", "parallel", "arbitrary")),
        name="mlstm",
    )(bias, hm3, hm3, hm3, g_row, g_row, g_col, g_col)


def _ml_out_kernel(hf_ref, hb_ref, mo_ref, g_ref, o_ref):
    hsum = hf_ref[...] + hb_ref[...]
    ms = jnp.mean(hsum * hsum, axis=-1, keepdims=True)
    hn = hsum * lax.rsqrt(ms + RMS_EPS) * g_ref[...]
    o_ref[...] = (hn * jax.nn.sigmoid(mo_ref[...].astype(F32))).astype(o_ref.dtype)


def _ml_out(hdir, hm, ml_gain, *, tm=512):
    T = hm.shape[0]
    mo0 = OFF_MO // ML_V_DIM
    return pl.pallas_call(
        _ml_out_kernel,
        out_shape=jax.ShapeDtypeStruct((T, ML_WIDTH), BF16),
        grid=(T // tm, ML_HEADS),
        in_specs=[pl.BlockSpec((None, tm, ML_V_DIM), lambda i, h: (0, i, h)),
                  pl.BlockSpec((None, tm, ML_V_DIM), lambda i, h: (1, i, h)),
                  pl.BlockSpec((tm, ML_V_DIM), lambda i, h: (i, mo0 + h)),
                  pl.BlockSpec((None, 1, ML_V_DIM), lambda i, h: (h, 0, 0))],
        out_specs=pl.BlockSpec((tm, ML_V_DIM), lambda i, h: (i, h)),
        compiler_params=_params(("parallel", "arbitrary")),
        name="mlstm_out",
    )(hdir, hdir, hm, ml_gain)


def _layer_norm_rows(z, g, b):
    zc = z - jnp.mean(z, axis=-1, keepdims=True)
    var = jnp.mean(zc * zc, axis=-1, keepdims=True)
    return zc * lax.rsqrt(var + LN_EPS) * g + b


def _proj_ln_kernel(*refs, n_in, alpha):
    a_refs = refs[:n_in]
    w_refs = refs[n_in:2 * n_in]
    x_ref, g_ref, b_ref, o_ref = refs[2 * n_in:]
    acc = alpha * x_ref[...]
    for a_ref, w_ref in zip(a_refs, w_refs):
        acc = acc + jnp.dot(a_ref[...], w_ref[...], preferred_element_type=F32)
    o_ref[...] = _layer_norm_rows(acc, g_ref[...], b_ref[...])


def _proj_ln(a_list, w_list, x, g, b, *, alpha, name, bm=256):
    T, D = x.shape
    n_in = len(a_list)
    in_specs = ([pl.BlockSpec((bm, a.shape[1]), lambda i: (i, 0)) for a in a_list]
                + [pl.BlockSpec(w.shape, lambda i: (0, 0)) for w in w_list]
                + [pl.BlockSpec((bm, D), lambda i: (i, 0)),
                   pl.BlockSpec((1, D), lambda i: (0, 0)),
                   pl.BlockSpec((1, D), lambda i: (0, 0))])
    return pl.pallas_call(
        functools.partial(_proj_ln_kernel, n_in=n_in, alpha=alpha),
        out_shape=jax.ShapeDtypeStruct((T, D), F32),
        grid=(T // bm,),
        in_specs=in_specs,
        out_specs=pl.BlockSpec((bm, D), lambda i: (i, 0)),
        compiler_params=_params(("parallel",)),
        name=name,
    )(*a_list, *w_list, x, g.reshape(1, D), b.reshape(1, D))


def _res_ln_kernel(x_ref, y_ref, g_ref, b_ref, o_ref, *, alpha):
    o_ref[...] = _layer_norm_rows(alpha * x_ref[...] + y_ref[...], g_ref[...], b_ref[...])


def _res_ln(x, y, g, b, *, alpha, name, bm=512):
    T, D = x.shape
    return pl.pallas_call(
        functools.partial(_res_ln_kernel, alpha=alpha),
        out_shape=jax.ShapeDtypeStruct((T, D), F32),
        grid=(T // bm,),
        in_specs=[pl.BlockSpec((bm, D), lambda i: (i, 0)),
                  pl.BlockSpec((bm, D), lambda i: (i, 0)),
                  pl.BlockSpec((1, D), lambda i: (0, 0)),
                  pl.BlockSpec((1, D), lambda i: (0, 0))],
        out_specs=pl.BlockSpec((bm, D), lambda i: (i, 0)),
        compiler_params=_params(("parallel",)),
        name=name,
    )(x, y, g.reshape(1, D), b.reshape(1, D))


def _xattn_kernel(q_ref, k_ref, v_ref, o_ref):
    hd = q_ref.shape[1] // XA_HEADS
    for h in range(XA_HEADS):
        sl = slice(h * hd, (h + 1) * hd)
        s = lax.dot_general(q_ref[:, sl], k_ref[:, sl], (((1,), (1,)), ((), ())),
                            preferred_element_type=F32)
        m = jnp.max(s, axis=-1, keepdims=True)
        p = jnp.exp(s - m)
        l = jnp.sum(p, axis=-1, keepdims=True)
        o = jnp.dot(p.astype(BF16), v_ref[:, sl], preferred_element_type=F32)
        o_ref[:, sl] = (o / l).astype(o_ref.dtype)


def _xattn(q3, k3, v3, *, tq=512):
    B, S, D = q3.shape
    M = k3.shape[1]
    return pl.pallas_call(
        _xattn_kernel,
        out_shape=jax.ShapeDtypeStruct((B, S, D), BF16),
        grid=(B, S // tq),
        in_specs=[pl.BlockSpec((None, tq, D), lambda b, i: (b, i, 0)),
                  pl.BlockSpec((None, M, D), lambda b, i: (b, 0, 0)),
                  pl.BlockSpec((None, M, D), lambda b, i: (b, 0, 0))],
        out_specs=pl.BlockSpec((None, tq, D), lambda b, i: (b, i, 0)),
        compiler_params=_params(("parallel", "arbitrary")),
        name="mem_xattn",
    )(q3, k3, v3)


def _oddeven_merge_sort(n):
    pairs = []

    def merge(lo, hi, r):
        step = r * 2
        if step < hi - lo:
            merge(lo, hi, step)
            merge(lo + r, hi, step)
            for i in range(lo + r, hi - r, step):
                pairs.append((i, i + r))
        else:
            pairs.append((lo, lo + r))

    def sort(lo, hi):
        if hi - lo >= 1:
            mid = lo + (hi - lo) // 2
            sort(lo, mid)
            sort(mid + 1, hi)
            merge(lo, hi, 1)

    sort(0, n - 1)
    return pairs


def _bitonic_merge(n):
    pairs = []
    k = n // 2
    while k >= 1:
        pairs.extend((i, i | k) for i in range(n) if i & k == 0)
        k //= 2
    return pairs


_SORT_TOPK = _oddeven_merge_sort(PEER_TOPK)
_BITONIC_TOPK = _bitonic_merge(PEER_TOPK)


def _ranks_before(a, b):
    return jnp.where(a[0] == b[0], a[1] - b[1], b[0] - a[0]) > 0


def _first(a, b):
    swap = _ranks_before(a, b)
    return (jnp.maximum(a[0], b[0]),) + tuple(jnp.where(swap, fb, fa) for fa, fb in zip(a[1:], b[1:]))


def _exchange(a, b):
    swap = _ranks_before(a, b)
    first = (jnp.maximum(a[0], b[0]),) + tuple(jnp.where(swap, fb, fa) for fa, fb in zip(a[1:], b[1:]))
    second = (jnp.minimum(a[0], b[0]),) + tuple(jnp.where(swap, fa, fb) for fa, fb in zip(a[1:], b[1:]))
    return first, second


def _run_network(network, elements):
    elements = list(elements)
    for i, j in network:
        elements[i], elements[j] = _exchange(elements[i], elements[j])
    return elements


def _merge_top(a, b, *, sort):
    k = PEER_TOPK
    c = [(_first(a[i], b[k - 1 - i]) if k - 1 - i < len(b) else a[i]) for i in range(k)]
    return _run_network(_BITONIC_TOPK, c) if sort else c


def _top_keys(scores):
    k = PEER_TOPK
    groups = [_run_network(_SORT_TOPK, [(scores[g * k + r], float(g * k + r)) for r in range(k)])
              for g in range(len(scores) // k)]
    while len(groups) > 1:
        groups = [_merge_top(groups[i], groups[i + 1], sort=True) for i in range(0, len(groups), 2)]
    return groups[0]


def _top_pairs(s0, s1):
    k = PEER_TOPK
    assert k == 16

    def el(a, b):
        return (s0[a][0] + s1[b][0], float(a * k + b), s0[a][1] * float(PEER_NKEYS) + s1[b][1])

    def row(a):
        return [el(a, b) for b in range(k // (a + 1))]

    col_tail = [el(a, 0) for a in range(k // 2, k)]
    g2 = _run_network(_BITONIC_TOPK, row(1) + col_tail[::-1])
    g3 = _run_network(_SORT_TOPK, row(2) + row(3) + row(4) + row(5) + row(6))
    t1 = _merge_top(row(0), g2, sort=True)
    t2 = _merge_top(g3, row(7), sort=True)
    return _merge_top(t1, t2, sort=False)


def _peer_route_kernel(q_ref, kexp_ref, e_ref, g_ref):
    tops = []
    for c in range(2):
        q_t = jnp.concatenate([q_ref[a * LANES:(a + 1) * LANES, c * PEER_NKEYS:(c + 1) * PEER_NKEYS]
                               for a in range(SUBLANES)], axis=1)
        s = lax.dot_general(kexp_ref[c], q_t, (((1,), (1,)), ((), ())), preferred_element_type=F32)
        tops.append(_top_keys([s[k * SUBLANES:(k + 1) * SUBLANES, :] for k in range(PEER_NKEYS)]))
    best = _top_pairs(tops[0], tops[1])
    m = functools.reduce(jnp.maximum, [b[0] for b in best])
    p = [jnp.exp(b[0] - m) for b in best]
    inv = 1.0 / functools.reduce(lambda x, y: x + y, p)
    for r in range(PEER_TOPK):
        g_ref[r] = p[r] * inv
        e_ref[r] = best[r][2].astype(jnp.int32)


def _expand_subkeys(subkeys):
    h, c, k, d = subkeys.shape
    eye = jnp.eye(SUBLANES, dtype=subkeys.dtype)
    return jnp.einsum('hckd,ab->hckabd', subkeys, eye).reshape(h, c, k * SUBLANES, SUBLANES * d)


def _peer_route(q, kexp):
    T = q.shape[0]
    tile = SUBLANES * LANES
    heads = kexp.shape[0]
    shape = (T // tile, heads, PEER_TOPK, SUBLANES, LANES)
    out_spec = pl.BlockSpec((None, None, PEER_TOPK, SUBLANES, LANES), lambda h, i: (i, h, 0, 0, 0))
    return pl.pallas_call(
        _peer_route_kernel,
        out_shape=(jax.ShapeDtypeStruct(shape, jnp.int32), jax.ShapeDtypeStruct(shape, F32)),
        grid=(heads, T // tile),
        in_specs=[pl.BlockSpec((tile, 2 * PEER_NKEYS), lambda h, i: (i, h)),
                  pl.BlockSpec((None,) + kexp.shape[1:], lambda h, i: (h, 0, 0, 0))],
        out_specs=(out_spec, out_spec),
        compiler_params=_params(("parallel", "arbitrary")),
        name="peer_route",
    )(q, kexp)


def _gelu(x):
    return 0.5 * x * (1.0 + lax.erf(x * (1.0 / math.sqrt(2.0))))


def _pack_expert_tables(u, v):
    half = u.shape[1] // 2

    def pack(a):
        b = lax.bitcast_convert_type(a.astype(BF16), jnp.uint16).astype(jnp.uint32)
        return b[:, :half] | (b[:, half:] << 16)

    return jnp.concatenate([pack(u), pack(v)], axis=1).reshape(u.shape[0], 1, 2 * half)


def _unpack_bf16_pair(w):
    lo = lax.bitcast_convert_type(w << 16, F32)
    hi = lax.bitcast_convert_type(w & jnp.uint32(0xFFFF0000), F32)
    return lo, hi


def _issue_rows(e_ref, t, table_hbm, dst, sem, lo, hi):
    for j in range(lo, hi):
        pltpu.async_copy(table_hbm.at[e_ref[t, j]], dst.at[pl.ds(t * PEER_SLOTS + j, 1)], sem,
                         priority=j % 2)


def _wait_rows(table_hbm, dst, sem):
    pltpu.make_async_copy(table_hbm.at[pl.ds(0, dst.shape[0]), 0], dst, sem).wait()


def _peer_gather_step(e_next_ref, x_ref, g_ref, table_hbm, o_ref, cur, nxt, sem_cur, sem_nxt):
    tt, d = x_ref.shape
    half = d // 2
    n_chunks = half // LANES
    first = PEER_SLOTS // 2
    _wait_rows(table_hbm, cur, sem_cur)
    lane = lax.broadcasted_iota(jnp.int32, (PEER_SLOTS, tt), 1)
    hmat = jnp.zeros((PEER_SLOTS, tt), F32)
    for t in range(tt):
        _issue_rows(e_next_ref, t, table_hbm, nxt, sem_nxt, 0, first)
        r0 = t * PEER_SLOTS
        acc = None
        for c in range(n_chunks):
            lo, hi = _unpack_bf16_pair(cur[r0:r0 + PEER_SLOTS, c * LANES:(c + 1) * LANES])
            term = (lo * x_ref[t:t + 1, c * LANES:(c + 1) * LANES]
                    + hi * x_ref[t:t + 1, half + c * LANES:half + (c + 1) * LANES])
            acc = term if acc is None else acc + term
        hmat = jnp.where(lane == t, jnp.sum(acc, axis=1, keepdims=True), hmat)
    a = _gelu(hmat) * g_ref[...]
    for t in range(tt):
        _issue_rows(e_next_ref, t, table_hbm, nxt, sem_nxt, first, PEER_SLOTS)
        r0 = t * PEER_SLOTS
        a_t = a[:, t:t + 1]
        for c in range(n_chunks):
            lo, hi = _unpack_bf16_pair(cur[r0:r0 + PEER_SLOTS, half + c * LANES:half + (c + 1) * LANES])
            o_ref[t:t + 1, c * LANES:(c + 1) * LANES] = jnp.sum(lo * a_t, axis=0, keepdims=True)
            o_ref[t:t + 1, half + c * LANES:half + (c + 1) * LANES] = jnp.sum(hi * a_t, axis=0, keepdims=True)


def _peer_gather_kernel(e_first_ref, e_next_ref, x_ref, g_ref, table_hbm, o_ref, buf0, buf1, sem, *, n_steps):
    i = pl.program_id(0)
    tt = x_ref.shape[0]

    @pl.when(i == 0)
    def _():
        for t in range(tt):
            _issue_rows(e_first_ref, t, table_hbm, buf0, sem.at[0], 0, PEER_SLOTS)

    @pl.when(i % 2 == 0)
    def _():
        _peer_gather_step(e_next_ref, x_ref, g_ref, table_hbm, o_ref, buf0, buf1, sem.at[0], sem.at[1])

    @pl.when(i % 2 == 1)
    def _():
        _peer_gather_step(e_next_ref, x_ref, g_ref, table_hbm, o_ref, buf1, buf0, sem.at[1], sem.at[0])

    @pl.when(i == n_steps - 1)
    def _():
        @pl.when(i % 2 == 0)
        def _():
            _wait_rows(table_hbm, buf1, sem.at[1])

        @pl.when(i % 2 == 1)
        def _():
            _wait_rows(table_hbm, buf0, sem.at[0])


def _peer_gather(x, experts, gates_tiled, table, *, tt=GATHER_TOKENS):
    T, D = x.shape
    n_steps = T // tt
    return pl.pallas_call(
        functools.partial(_peer_gather_kernel, n_steps=n_steps),
        out_shape=jax.ShapeDtypeStruct((T, D), F32),
        grid=(n_steps,),
        in_specs=[pl.BlockSpec((tt, PEER_SLOTS), lambda i: (i, 0), memory_space=pltpu.SMEM),
                  pl.BlockSpec((tt, PEER_SLOTS), lambda i: (jnp.minimum(i + 1, n_steps - 1), 0),
                               memory_space=pltpu.SMEM),
                  pl.BlockSpec((tt, D), lambda i: (i, 0)),
                  pl.BlockSpec((None, PEER_SLOTS, tt), lambda i: (i, 0, 0)),
                  pl.BlockSpec(memory_space=pl.ANY)],
        out_specs=pl.BlockSpec((tt, D), lambda i: (i, 0)),
        scratch_shapes=[pltpu.VMEM((tt * PEER_SLOTS, D), jnp.uint32),
                        pltpu.VMEM((tt * PEER_SLOTS, D), jnp.uint32),
                        pltpu.SemaphoreType.DMA((2,))],
        compiler_params=_params(("arbitrary",)),
        name="peer_gather",
    )(experts, experts, x, gates_tiled, table)


def _rope_tables(S):
    pos = jnp.arange(S, dtype=jnp.int32)
    row = (pos // GRID_W).astype(F32)
    col = (pos % GRID_W).astype(F32)
    n_freq = ATT_HEAD_DIM // 4
    inv_freq = ROPE_THETA ** (-jnp.arange(n_freq, dtype=F32) / n_freq)
    ang_r = row[:, None] * inv_freq
    ang_c = col[:, None] * inv_freq
    cos = jnp.concatenate([jnp.cos(ang_r), jnp.cos(ang_r), jnp.cos(ang_c), jnp.cos(ang_c)], axis=-1)
    sin = jnp.concatenate([-jnp.sin(ang_r), jnp.sin(ang_r), -jnp.sin(ang_c), jnp.sin(ang_c)], axis=-1)
    return cos, sin


def _layer(x, mem, w_in, b_igate, b_fgate, att_q_norm, att_k_norm, ml_norm, w_out, ln1_g, ln1_b,
           xa_wq, xa_wk, xa_wv, xa_wo, ln2_g, ln2_b, peer_wq, peer_subkeys, peer_u, peer_v,
           ln3_g, ln3_b, *, alpha):
    B, S, D = x.shape
    T = B * S
    M = mem.shape[1]
    xt = x.reshape(T, D)

    hm = _matmul(xt, w_in[:, :MAIN_COLS].astype(BF16), bm=512, bn=512, out_dtype=BF16, name="in_proj")
    w_gate = jnp.pad(w_in[:, MAIN_COLS:], ((0, 0), (0, LANES - N_GATES))).astype(BF16)
    gates = _matmul(xt, w_gate, bm=512, bn=LANES, out_dtype=F32, name="gate_proj")
    hm3 = hm.reshape(B, S, MAIN_COLS)

    cos, sin = _rope_tables(S)
    gains = jnp.concatenate([
        jnp.broadcast_to(att_q_norm * (ATT_HEAD_DIM ** -0.5), (ATT_HEADS, ATT_HEAD_DIM)),
        jnp.broadcast_to(att_k_norm, (ATT_KV_HEADS, ATT_HEAD_DIM)),
        jnp.zeros((16 - ATT_HEADS - ATT_KV_HEADS, ATT_HEAD_DIM), F32)], axis=0)
    qk = _qk_prep(hm, cos, sin, gains, S=S)
    att = _attention(qk.reshape(B, S, -1), hm3).reshape(T, ATT_WIDTH)

    gates_t = jnp.transpose(gates[:, :N_GATES].reshape(B, S, N_GATES), (0, 2, 1))
    bias = jnp.concatenate([b_igate.reshape(-1), b_fgate.reshape(-1)]).astype(F32)
    hdir = _mlstm(hm3, gates_t, bias).reshape(2, T, ML_WIDTH)
    ml = _ml_out(hdir, hm, ml_norm.reshape(ML_HEADS, 1, ML_V_DIM))

    w_out_b = w_out.astype(BF16)
    x1 = _proj_ln([att, ml], [w_out_b[:ATT_WIDTH], w_out_b[ATT_WIDTH:]], xt, ln1_g, ln1_b,
                  alpha=alpha, name="out_proj_ln1")

    hd = D // XA_HEADS
    xq = _matmul(x1, xa_wq.astype(BF16), bm=512, bn=512, out_dtype=BF16, name="xa_q", scale=hd ** -0.5)
    memt = mem.reshape(B * M, D)
    xk = _matmul(memt, xa_wk.astype(BF16), bm=256, bn=512, out_dtype=BF16, name="xa_k")
    xv = _matmul(memt, xa_wv.astype(BF16), bm=256, bn=512, out_dtype=BF16, name="xa_v")
    xo = _xattn(xq.reshape(B, S, D), xk.reshape(B, M, D), xv.reshape(B, M, D)).reshape(T, D)
    x2 = _proj_ln([xo], [xa_wo.astype(BF16)], x1, ln2_g, ln2_b, alpha=alpha, name="xa_out_ln2")

    pq = _matmul(x2, peer_wq.astype(BF16), bm=512, bn=512, out_dtype=BF16, name="peer_q")
    e5, g5 = _peer_route(pq, _expand_subkeys(peer_subkeys.astype(BF16)))
    experts = jnp.transpose(e5, (0, 3, 4, 1, 2)).reshape(T, PEER_SLOTS)
    gates = jnp.transpose(g5, (0, 3, 4, 1, 2)).reshape(T, PEER_SLOTS)
    gates_tiled = jnp.transpose(gates.reshape(T // GATHER_TOKENS, GATHER_TOKENS, PEER_SLOTS), (0, 2, 1))
    y = _peer_gather(x2, experts, gates_tiled, _pack_expert_tables(peer_u, peer_v))
    x3 = _res_ln(x2, y, ln3_g, ln3_b, alpha=alpha, name="peer_ln3")
    return x3.reshape(B, S, D)


def kernel(x, mem, w_in, b_igate, b_fgate, att_q_norm, att_k_norm, ml_norm, w_out, ln1_g, ln1_b,
           xa_wq, xa_wk, xa_wv, xa_wo, ln2_g, ln2_b, peer_wq, peer_subkeys, peer_u, peer_v,
           ln3_g, ln3_b):
    depth = w_in.shape[0]
    alpha = (2.0 * depth) ** 0.25
    for l in range(depth):
        x = _layer(x, mem, w_in[l], b_igate[l], b_fgate[l], att_q_norm[l], att_k_norm[l], ml_norm[l],
                   w_out[l], ln1_g[l], ln1_b[l], xa_wq[l], xa_wk[l], xa_wv[l], xa_wo[l], ln2_g[l],
                   ln2_b[l], peer_wq[l], peer_subkeys[l], peer_u[l], peer_v[l], ln3_g[l], ln3_b[l],
                   alpha=alpha)
    return x
```

```python
import functools
import math

import jax
import jax.numpy as jnp
from jax import lax
from jax.experimental import pallas as pl
from jax.experimental.pallas import tpu as pltpu

F32 = jnp.float32
BF16 = jnp.bfloat16

GRID_W = 64
ATT_HEAD_DIM = 128
ATT_HEADS = 8
ATT_KV_HEADS = 2
ATT_WIDTH = ATT_HEADS * ATT_HEAD_DIM
ROPE_THETA = 10000.0
ML_HEADS = 4
ML_QK_DIM = 128
ML_V_DIM = 256
ML_WIDTH = ML_HEADS * ML_V_DIM
XA_HEADS = 4
PEER_HEADS = 8
PEER_NKEYS = 128
PEER_TOPK = 16
PEER_SLOTS = PEER_HEADS * PEER_TOPK
LN_EPS = 1e-5
RMS_EPS = 1e-6

OFF_AQ = 0
OFF_AK = OFF_AQ + ATT_WIDTH
OFF_AV = OFF_AK + ATT_KV_HEADS * ATT_HEAD_DIM
OFF_MQ = OFF_AV + ATT_KV_HEADS * ATT_HEAD_DIM
OFF_MK = OFF_MQ + ML_HEADS * ML_QK_DIM
OFF_MV = OFF_MK + ML_HEADS * ML_QK_DIM
OFF_MO = OFF_MV + ML_WIDTH
MAIN_COLS = OFF_MO + ML_WIDTH
N_GATES = 4 * ML_HEADS

LANES = 128
SUBLANES = 8
VMEM_LIMIT = 48 * 1024 * 1024

ML_CHUNK = 256
GATHER_TOKENS = 8


def _params(sem, vmem=VMEM_LIMIT):
    return pltpu.CompilerParams(dimension_semantics=sem, vmem_limit_bytes=vmem)


def _mm_kernel(a_ref, w_ref, o_ref, ab_ref, *, scale):
    @pl.when(pl.program_id(1) == 0)
    def _():
        ab_ref[...] = a_ref[...].astype(BF16)

    acc = jnp.dot(ab_ref[...], w_ref[...], preferred_element_type=F32)
    if scale != 1.0:
        acc = acc * scale
    o_ref[...] = acc.astype(o_ref.dtype)


def _matmul(a, w, *, bm, bn, out_dtype, name, scale=1.0):
    M, K = a.shape
    N = w.shape[1]
    assert M % bm == 0 and N % bn == 0
    return pl.pallas_call(
        functools.partial(_mm_kernel, scale=scale),
        out_shape=jax.ShapeDtypeStruct((M, N), out_dtype),
        grid=(M // bm, N // bn),
        in_specs=[pl.BlockSpec((bm, K), lambda i, j: (i, 0)),
                  pl.BlockSpec((K, bn), lambda i, j: (0, j))],
        out_specs=pl.BlockSpec((bm, bn), lambda i, j: (i, j)),
        scratch_shapes=[pltpu.VMEM((bm, K), BF16)],
        compiler_params=_params(("parallel", "arbitrary")),
        name=name,
    )(a, w)


def _qk_prep_kernel(h_ref, cos_ref, sin_ref, g_ref, o_ref):
    j = pl.program_id(1)
    z = h_ref[...].astype(F32)
    ms = jnp.mean(z * z, axis=-1, keepdims=True)
    zn = z * lax.rsqrt(ms + RMS_EPS) * g_ref[pl.ds(j, 1), :]
    lane = lax.broadcasted_iota(jnp.int32, zn.shape, 1)
    partner = jnp.where((lane % 64) < 32, pltpu.roll(zn, 96, 1), pltpu.roll(zn, 32, 1))
    o_ref[...] = (zn * cos_ref[...] + partner * sin_ref[...]).astype(o_ref.dtype)


def _qk_prep(hm, cos, sin, gains, *, S, tm=512):
    T = hm.shape[0]
    n_heads = ATT_HEADS + ATT_KV_HEADS
    s_blocks = S // tm
    return pl.pallas_call(
        _qk_prep_kernel,
        out_shape=jax.ShapeDtypeStruct((T, n_heads * ATT_HEAD_DIM), BF16),
        grid=(T // tm, n_heads),
        in_specs=[pl.BlockSpec((tm, LANES), lambda i, j: (i, j)),
                  pl.BlockSpec((tm, LANES), lambda i, j: (i % s_blocks, 0)),
                  pl.BlockSpec((tm, LANES), lambda i, j: (i % s_blocks, 0)),
                  pl.BlockSpec((16, LANES), lambda i, j: (0, 0))],
        out_specs=pl.BlockSpec((tm, LANES), lambda i, j: (i, j)),
        compiler_params=_params(("parallel", "arbitrary")),
        name="qk_prep",
    )(hm, cos, sin, gains)


def _attn_kernel(q_ref, k_ref, v_ref, o_ref):
    s = lax.dot_general(q_ref[...], k_ref[...], (((1,), (1,)), ((), ())),
                        preferred_element_type=F32)
    m = jnp.max(s, axis=-1, keepdims=True)
    p = jnp.exp(s - m)
    l = jnp.sum(p, axis=-1, keepdims=True)
    o = jnp.dot(p.astype(BF16), v_ref[...], preferred_element_type=F32)
    o_ref[...] = (o / l).astype(o_ref.dtype)


def _attention(qk3, hm3, *, tq=256):
    B, S, _ = qk3.shape
    group = ATT_HEADS // ATT_KV_HEADS
    k_blk0 = ATT_HEADS
    v_blk0 = OFF_AV // LANES
    return pl.pallas_call(
        _attn_kernel,
        out_shape=jax.ShapeDtypeStruct((B, S, ATT_WIDTH), BF16),
        grid=(B, ATT_HEADS, S // tq),
        in_specs=[pl.BlockSpec((None, tq, LANES), lambda b, h, i: (b, i, h)),
                  pl.BlockSpec((None, S, LANES), lambda b, h, i: (b, 0, k_blk0 + h // group)),
                  pl.BlockSpec((None, S, LANES), lambda b, h, i: (b, 0, v_blk0 + h // group))],
        out_specs=pl.BlockSpec((None, tq, LANES), lambda b, h, i: (b, i, h)),
        compiler_params=_params(("parallel", "parallel", "arbitrary")),
        name="gqa_attention",
    )(qk3, qk3, hm3)


def _log_sigmoid(x):
    return -(jnp.maximum(-x, 0.0) + jnp.log1p(jnp.exp(-jnp.abs(x))))


def _mlstm_kernel(bias_ref, q_ref, k_ref, v_ref, gi_row_ref, gf_row_ref, gi_col_ref, gf_col_ref,
                  o_ref, c_ref, n_ref, m_ref):
    h = pl.program_id(1)
    d = pl.program_id(2)
    c = pl.program_id(3)
    L = q_ref.shape[0]

    @pl.when(c == 0)
    def _():
        c_ref[...] = jnp.zeros_like(c_ref)
        n_ref[...] = jnp.zeros_like(n_ref)
        m_ref[...] = jnp.zeros_like(m_ref)

    b_i = bias_ref[d * ML_HEADS + h]
    b_f = bias_ref[2 * ML_HEADS + d * ML_HEADS + h]
    i_row = gi_row_ref[...] + b_i
    i_col = gi_col_ref[...] + b_i
    f_row = _log_sigmoid(gf_row_ref[...] + b_f)
    f_col = _log_sigmoid(gf_col_ref[...] + b_f)

    t_idx = lax.broadcasted_iota(jnp.int32, (L, L), 0)
    s_idx = lax.broadcasted_iota(jnp.int32, (L, L), 1)
    sgn = 1 - 2 * d
    within = (t_idx - s_idx) * sgn >= 0
    within_t = (s_idx - t_idx) * sgn >= 0
    b_col = jnp.sum(jnp.where(within, f_row, 0.0), axis=1, keepdims=True)
    b_row = jnp.sum(jnp.where(within_t, f_col, 0.0), axis=0, keepdims=True)
    b_tot = jnp.sum(f_row, axis=1, keepdims=True)

    m_prev = m_ref[...]
    log_d = jnp.where(within, b_col - b_row + i_row, -jnp.inf)
    log_inter = b_col + m_prev
    m_row = jnp.maximum(log_inter, jnp.max(log_d, axis=1, keepdims=True))
    dmat = jnp.exp(log_d - m_row)
    w_inter = jnp.exp(log_inter - m_row)

    q = q_ref[...]
    k = k_ref[...]
    v = v_ref[...]
    k_scale = ML_QK_DIM ** -0.5
    qk = lax.dot_general(q, k, (((1,), (1,)), ((), ())), preferred_element_type=F32)
    qk = qk * k_scale * dmat
    c_prev = c_ref[...]
    n_prev = n_ref[...]
    num = (jnp.dot(qk.astype(BF16), v, preferred_element_type=F32)
           + w_inter * jnp.dot(q, c_prev.astype(BF16), preferred_element_type=F32))
    den = (jnp.sum(qk, axis=1, keepdims=True)
           + w_inter * jnp.sum(q.astype(F32) * n_prev, axis=1, keepdims=True))
    o_ref[...] = (num / jnp.maximum(jnp.abs(den), jnp.exp(-m_row))).astype(o_ref.dtype)

    log_w = b_tot - b_col + i_col
    m_new = jnp.maximum(b_tot + m_prev, jnp.max(log_w, axis=0, keepdims=True))
    w = jnp.exp(log_w - m_new) * k_scale
    decay = jnp.exp(b_tot + m_prev - m_new)
    kw = k.astype(F32) * w
    c_ref[...] = decay * c_prev + lax.dot_general(
        kw.astype(BF16), v, (((0,), (0,)), ((), ())), preferred_element_type=F32)
    n_ref[...] = decay * n_prev + jnp.sum(kw, axis=0, keepdims=True)
    m_ref[...] = m_new


def _mlstm(hm3, gates_t, bias, *, L=ML_CHUNK):
    B, S, _ = hm3.shape
    nc = S // L
    g_row = gates_t.reshape(B, N_GATES, 1, S)
    g_col = gates_t.reshape(B, N_GATES, S, 1)

    def cidx(d, c):
        return c + d * (nc - 1 - 2 * c)

    q0, k0, v0 = OFF_MQ // ML_QK_DIM, OFF_MK // ML_QK_DIM, OFF_MV // ML_V_DIM
    return pl.pallas_call(
        _mlstm_kernel,
        out_shape=jax.ShapeDtypeStruct((2, B, S, ML_WIDTH), F32),
        grid=(B, ML_HEADS, 2, nc),
        in_specs=[
            pl.BlockSpec(memory_space=pltpu.SMEM),
            pl.BlockSpec((None, L, ML_QK_DIM), lambda b, h, d, c: (b, cidx(d, c), q0 + h)),
            pl.BlockSpec((None, L, ML_QK_DIM), lambda b, h, d, c: (b, cidx(d, c), k0 + h)),
            pl.BlockSpec((None, L, ML_V_DIM), lambda b, h, d, c: (b, cidx(d, c), v0 + h)),
            pl.BlockSpec((None, None, 1, L), lambda b, h, d, c: (b, d * ML_HEADS + h, 0, cidx(d, c))),
            pl.BlockSpec((None, None, 1, L),
                         lambda b, h, d, c: (b, 2 * ML_HEADS + d * ML_HEADS + h, 0, cidx(d, c))),
            pl.BlockSpec((None, None, L, 1), lambda b, h, d, c: (b, d * ML_HEADS + h, cidx(d, c), 0)),
            pl.BlockSpec((None, None, L, 1),
                         lambda b, h, d, c: (b, 2 * ML_HEADS + d * ML_HEADS + h, cidx(d, c), 0)),
        ],
        out_specs=pl.BlockSpec((None, None, L, ML_V_DIM), lambda b, h, d, c: (d, b, cidx(d, c), h)),
        scratch_shapes=[pltpu.VMEM((ML_QK_DIM, ML_V_DIM), F32),
                        pltpu.VMEM((1, ML_QK_DIM), F32),
                        pltpu.VMEM((1, 1), F32)],
        compiler_params=_params(("parallel", "parallel", "parallel", "arbitrary")),
        name="mlstm",
    )(bias, hm3, hm3, hm3, g_row, g_row, g_col, g_col)


def _ml_out_kernel(hf_ref, hb_ref, mo_ref, g_ref, o_ref):
    hsum = hf_ref[...] + hb_ref[...]
    ms = jnp.mean(hsum * hsum, axis=-1, keepdims=True)
    hn = hsum * lax.rsqrt(ms + RMS_EPS) * g_ref[...]
    o_ref[...] = (hn * jax.nn.sigmoid(mo_ref[...].astype(F32))).astype(o_ref.dtype)


def _ml_out(hdir, hm, ml_gain, *, tm=512):
    T = hm.shape[0]
    mo0 = OFF_MO // ML_V_DIM
    return pl.pallas_call(
        _ml_out_kernel,
        out_shape=jax.ShapeDtypeStruct((T, ML_WIDTH), BF16),
        grid=(T // tm, ML_HEADS),
        in_specs=[pl.BlockSpec((None, tm, ML_V_DIM), lambda i, h: (0, i, h)),
                  pl.BlockSpec((None, tm, ML_V_DIM), lambda i, h: (1, i, h)),
                  pl.BlockSpec((tm, ML_V_DIM), lambda i, h: (i, mo0 + h)),
                  pl.BlockSpec((None, 1, ML_V_DIM), lambda i, h: (h, 0, 0))],
        out_specs=pl.BlockSpec((tm, ML_V_DIM), lambda i, h: (i, h)),
        compiler_params=_params(("parallel", "arbitrary")),
        name="mlstm_out",
    )(hdir, hdir, hm, ml_gain)


def _layer_norm_rows(z, g, b):
    zc = z - jnp.mean(z, axis=-1, keepdims=True)
    var = jnp.mean(zc * zc, axis=-1, keepdims=True)
    return zc * lax.rsqrt(var + LN_EPS) * g + b


def _proj_ln_kernel(*refs, n_in, alpha):
    a_refs = refs[:n_in]
    w_refs = refs[n_in:2 * n_in]
    x_ref, g_ref, b_ref, o_ref = refs[2 * n_in:]
    acc = alpha * x_ref[...]
    for a_ref, w_ref in zip(a_refs, w_refs):
        acc = acc + jnp.dot(a_ref[...], w_ref[...], preferred_element_type=F32)
    o_ref[...] = _layer_norm_rows(acc, g_ref[...], b_ref[...])


def _proj_ln(a_list, w_list, x, g, b, *, alpha, name, bm=256):
    T, D = x.shape
    n_in = len(a_list)
    in_specs = ([pl.BlockSpec((bm, a.shape[1]), lambda i: (i, 0)) for a in a_list]
                + [pl.BlockSpec(w.shape, lambda i: (0, 0)) for w in w_list]
                + [pl.BlockSpec((bm, D), lambda i: (i, 0)),
                   pl.BlockSpec((1, D), lambda i: (0, 0)),
                   pl.BlockSpec((1, D), lambda i: (0, 0))])
    return pl.pallas_call(
        functools.partial(_proj_ln_kernel, n_in=n_in, alpha=alpha),
        out_shape=jax.ShapeDtypeStruct((T, D), F32),
        grid=(T // bm,),
        in_specs=in_specs,
        out_specs=pl.BlockSpec((bm, D), lambda i: (i, 0)),
        compiler_params=_params(("parallel",)),
        name=name,
    )(*a_list, *w_list, x, g.reshape(1, D), b.reshape(1, D))


def _res_ln_kernel(x_ref, y_ref, g_ref, b_ref, o_ref, *, alpha):
    o_ref[...] = _layer_norm_rows(alpha * x_ref[...] + y_ref[...], g_ref[...], b_ref[...])


def _res_ln(x, y, g, b, *, alpha, name, bm=512):
    T, D = x.shape
    return pl.pallas_call(
        functools.partial(_res_ln_kernel, alpha=alpha),
        out_shape=jax.ShapeDtypeStruct((T, D), F32),
        grid=(T // bm,),
        in_specs=[pl.BlockSpec((bm, D), lambda i: (i, 0)),
                  pl.BlockSpec((bm, D), lambda i: (i, 0)),
                  pl.BlockSpec((1, D), lambda i: (0, 0)),
                  pl.BlockSpec((1, D), lambda i: (0, 0))],
        out_specs=pl.BlockSpec((bm, D), lambda i: (i, 0)),
        compiler_params=_params(("parallel",)),
        name=name,
    )(x, y, g.reshape(1, D), b.reshape(1, D))


def _xattn_kernel(q_ref, k_ref, v_ref, o_ref):
    hd = q_ref.shape[1] // XA_HEADS
    for h in range(XA_HEADS):
        sl = slice(h * hd, (h + 1) * hd)
        s = lax.dot_general(q_ref[:, sl], k_ref[:, sl], (((1,), (1,)), ((), ())),
                            preferred_element_type=F32)
        m = jnp.max(s, axis=-1, keepdims=True)
        p = jnp.exp(s - m)
        l = jnp.sum(p, axis=-1, keepdims=True)
        o = jnp.dot(p.astype(BF16), v_ref[:, sl], preferred_element_type=F32)
        o_ref[:, sl] = (o / l).astype(o_ref.dtype)


def _xattn(q3, k3, v3, *, tq=512):
    B, S, D = q3.shape
    M = k3.shape[1]
    return pl.pallas_call(
        _xattn_kernel,
        out_shape=jax.ShapeDtypeStruct((B, S, D), BF16),
        grid=(B, S // tq),
        in_specs=[pl.BlockSpec((None, tq, D), lambda b, i: (b, i, 0)),
                  pl.BlockSpec((None, M, D), lambda b, i: (b, 0, 0)),
                  pl.BlockSpec((None, M, D), lambda b, i: (b, 0, 0))],
        out_specs=pl.BlockSpec((None, tq, D), lambda b, i: (b, i, 0)),
        compiler_params=_params(("parallel", "arbitrary")),
        name="mem_xattn",
    )(q3, k3, v3)


def _oddeven_merge_sort(n):
    pairs = []

    def merge(lo, hi, r):
        step = r * 2
        if step < hi - lo:
            merge(lo, hi, step)
            merge(lo + r, hi, step)
            for i in range(lo + r, hi - r, step):
                pairs.append((i, i + r))
        else:
            pairs.append((lo, lo + r))

    def sort(lo, hi):
        if hi - lo >= 1:
            mid = lo + (hi - lo) // 2
            sort(lo, mid)
            sort(mid + 1, hi)
            merge(lo, hi, 1)

    sort(0, n - 1)
    return pairs


def _bitonic_merge(n):
    pairs = []
    k = n // 2
    while k >= 1:
        pairs.extend((i, i | k) for i in range(n) if i & k == 0)
        k //= 2
    return pairs


_SORT_TOPK = _oddeven_merge_sort(PEER_TOPK)
_BITONIC_TOPK = _bitonic_merge(PEER_TOPK)


def _ranks_before(a, b):
    return jnp.where(a[0] == b[0], a[1] - b[1], b[0] - a[0]) > 0


def _first(a, b):
    swap = _ranks_before(a, b)
    return (jnp.maximum(a[0], b[0]),) + tuple(jnp.where(swap, fb, fa) for fa, fb in zip(a[1:], b[1:]))


def _exchange(a, b):
    swap = _ranks_before(a, b)
    first = (jnp.maximum(a[0], b[0]),) + tuple(jnp.where(swap, fb, fa) for fa, fb in zip(a[1:], b[1:]))
    second = (jnp.minimum(a[0], b[0]),) + tuple(jnp.where(swap, fa, fb) for fa, fb in zip(a[1:], b[1:]))
    return first, second


def _run_network(network, elements):
    elements = list(elements)
    for i, j in network:
        elements[i], elements[j] = _exchange(elements[i], elements[j])
    return elements


def _merge_top(a, b, *, sort):
    k = PEER_TOPK
    c = [(_first(a[i], b[k - 1 - i]) if k - 1 - i < len(b) else a[i]) for i in range(k)]
    return _run_network(_BITONIC_TOPK, c) if sort else c


def _top_keys(scores):
    k = PEER_TOPK
    groups = [_run_network(_SORT_TOPK, [(scores[g * k + r], float(g * k + r)) for r in range(k)])
              for g in range(len(scores) // k)]
    while len(groups) > 1:
        groups = [_merge_top(groups[i], groups[i + 1], sort=True) for i in range(0, len(groups), 2)]
    return groups[0]


def _top_pairs(s0, s1):
    k = PEER_TOPK
    assert k == 16

    def el(a, b):
        return (s0[a][0] + s1[b][0], float(a * k + b), s0[a][1] * float(PEER_NKEYS) + s1[b][1])

    def row(a):
        return [el(a, b) for b in range(k // (a + 1))]

    col_tail = [el(a, 0) for a in range(k // 2, k)]
    g2 = _run_network(_BITONIC_TOPK, row(1) + col_tail[::-1])
    g3 = _run_network(_SORT_TOPK, row(2) + row(3) + row(4) + row(5) + row(6))
    t1 = _merge_top(row(0), g2, sort=True)
    t2 = _merge_top(g3, row(7), sort=True)
    return _merge_top(t1, t2, sort=False)


def _peer_route_kernel(q_ref, kexp_ref, e_ref, g_ref):
    tops = []
    for c in range(2):
        q_t = jnp.concatenate([q_ref[a * LANES:(a + 1) * LANES, c * PEER_NKEYS:(c + 1) * PEER_NKEYS]
                               for a in range(SUBLANES)], axis=1)
        s = lax.dot_general(kexp_ref[c], q_t, (((1,), (1,)), ((), ())), preferred_element_type=F32)
        tops.append(_top_keys([s[k * SUBLANES:(k + 1) * SUBLANES, :] for k in range(PEER_NKEYS)]))
    best = _top_pairs(tops[0], tops[1])
    m = functools.reduce(jnp.maximum, [b[0] for b in best])
    p = [jnp.exp(b[0] - m) for b in best]
    inv = 1.0 / functools.reduce(lambda x, y: x + y, p)
    for r in range(PEER_TOPK):
        g_ref[r] = p[r] * inv
        e_ref[r] = best[r][2].astype(jnp.int32)


def _expand_subkeys(subkeys):
    h, c, k, d = subkeys.shape
    eye = jnp.eye(SUBLANES, dtype=subkeys.dtype)
    return jnp.einsum('hckd,ab->hckabd', subkeys, eye).reshape(h, c, k * SUBLANES, SUBLANES * d)


def _peer_route(q, kexp):
    T = q.shape[0]
    tile = SUBLANES * LANES
    heads = kexp.shape[0]
    shape = (T // tile, heads, PEER_TOPK, SUBLANES, LANES)
    out_spec = pl.BlockSpec((None, None, PEER_TOPK, SUBLANES, LANES), lambda h, i: (i, h, 0, 0, 0))
    return pl.pallas_call(
        _peer_route_kernel,
        out_shape=(jax.ShapeDtypeStruct(shape, jnp.int32), jax.ShapeDtypeStruct(shape, F32)),
        grid=(heads, T // tile),
        in_specs=[pl.BlockSpec((tile, 2 * PEER_NKEYS), lambda h, i: (i, h)),
                  pl.BlockSpec((None,) + kexp.shape[1:], lambda h, i: (h, 0, 0, 0))],
        out_specs=(out_spec, out_spec),
        compiler_params=_params(("parallel", "arbitrary")),
        name="peer_route",
    )(q, kexp)


def _gelu(x):
    return 0.5 * x * (1.0 + lax.erf(x * (1.0 / math.sqrt(2.0))))


def _pack_expert_tables(u, v):
    half = u.shape[1] // 2

    def pack(a):
        b = lax.bitcast_convert_type(a.astype(BF16), jnp.uint16).astype(jnp.uint32)
        return b[:, :half] | (b[:, half:] << 16)

    return jnp.concatenate([pack(u), pack(v)], axis=1).reshape(u.shape[0], 1, 2 * half)


def _unpack_bf16_pair(w):
    lo = lax.bitcast_convert_type(w << 16, F32)
    hi = lax.bitcast_convert_type(w & jnp.uint32(0xFFFF0000), F32)
    return lo, hi


def _issue_rows(e_ref, t, table_hbm, dst, sem, lo, hi):
    for j in range(lo, hi):
        pltpu.async_copy(table_hbm.at[e_ref[t, j]], dst.at[pl.ds(t * PEER_SLOTS + j, 1)], sem,
                         priority=j % 2)


def _wait_rows(table_hbm, dst, sem):
    pltpu.make_async_copy(table_hbm.at[pl.ds(0, dst.shape[0]), 0], dst, sem).wait()


def _peer_gather_step(e_next_ref, x_ref, g_ref, table_hbm, o_ref, cur, nxt, sem_cur, sem_nxt):
    tt, d = x_ref.shape
    half = d // 2
    n_chunks = half // LANES
    _wait_rows(table_hbm, cur, sem_cur)
    lane = lax.broadcasted_iota(jnp.int32, (PEER_SLOTS, tt), 1)
    hmat = jnp.zeros((PEER_SLOTS, tt), F32)
    for t in range(tt):
        _issue_rows(e_next_ref, t, table_hbm, nxt, sem_nxt, 0, PEER_SLOTS)
        r0 = t * PEER_SLOTS
        acc = None
        for c in range(n_chunks):
            lo, hi = _unpack_bf16_pair(cur[r0:r0 + PEER_SLOTS, c * LANES:(c + 1) * LANES])
            term = (lo * x_ref[t:t + 1, c * LANES:(c + 1) * LANES]
                    + hi * x_ref[t:t + 1, half + c * LANES:half + (c + 1) * LANES])
            acc = term if acc is None else acc + term
        hmat = jnp.where(lane == t, jnp.sum(acc, axis=1, keepdims=True), hmat)
    a = _gelu(hmat) * g_ref[...]
    for t in range(tt):
        r0 = t * PEER_SLOTS
        a_t = a[:, t:t + 1]
        for c in range(n_chunks):
            lo, hi = _unpack_bf16_pair(cur[r0:r0 + PEER_SLOTS, half + c * LANES:half + (c + 1) * LANES])
            o_ref[t:t + 1, c * LANES:(c + 1) * LANES] = jnp.sum(lo * a_t, axis=0, keepdims=True)
            o_ref[t:t + 1, half + c * LANES:half + (c + 1) * LANES] = jnp.sum(hi * a_t, axis=0, keepdims=True)


def _peer_gather_kernel(e_first_ref, e_next_ref, x_ref, g_ref, table_hbm, o_ref, buf0, buf1, sem, *, n_steps):
    i = pl.program_id(0)
    tt = x_ref.shape[0]

    @pl.when(i == 0)
    def _():
        for t in range(tt):
            _issue_rows(e_first_ref, t, table_hbm, buf0, sem.at[0], 0, PEER_SLOTS)

    @pl.when(i % 2 == 0)
    def _():
        _peer_gather_step(e_next_ref, x_ref, g_ref, table_hbm, o_ref, buf0, buf1, sem.at[0], sem.at[1])

    @pl.when(i % 2 == 1)
    def _():
        _peer_gather_step(e_next_ref, x_ref, g_ref, table_hbm, o_ref, buf1, buf0, sem.at[1], sem.at[0])

    @pl.when(i == n_steps - 1)
    def _():
        @pl.when(i % 2 == 0)
        def _():
            _wait_rows(table_hbm, buf1, sem.at[1])

        @pl.when(i % 2 == 1)
        def _():
            _wait_rows(table_hbm, buf0, sem.at[0])


def _peer_gather(x, experts, gates_tiled, table, *, tt=GATHER_TOKENS):
    T, D = x.shape
    n_steps = T // tt
    return pl.pallas_call(
        functools.partial(_peer_gather_kernel, n_steps=n_steps),
        out_shape=jax.ShapeDtypeStruct((T, D), F32),
        grid=(n_steps,),
        in_specs=[pl.BlockSpec((tt, PEER_SLOTS), lambda i: (i, 0), memory_space=pltpu.SMEM),
                  pl.BlockSpec((tt, PEER_SLOTS), lambda i: (jnp.minimum(i + 1, n_steps - 1), 0),
                               memory_space=pltpu.SMEM),
                  pl.BlockSpec((tt, D), lambda i: (i, 0)),
                  pl.BlockSpec((None, PEER_SLOTS, tt), lambda i: (i, 0, 0)),
                  pl.BlockSpec(memory_space=pl.ANY)],
        out_specs=pl.BlockSpec((tt, D), lambda i: (i, 0)),
        scratch_shapes=[pltpu.VMEM((tt * PEER_SLOTS, D), jnp.uint32),
                        pltpu.VMEM((tt * PEER_SLOTS, D), jnp.uint32),
                        pltpu.SemaphoreType.DMA((2,))],
        compiler_params=_params(("arbitrary",)),
        name="peer_gather",
    )(experts, experts, x, gates_tiled, table)


def _rope_tables(S):
    pos = jnp.arange(S, dtype=jnp.int32)
    row = (pos // GRID_W).astype(F32)
    col = (pos % GRID_W).astype(F32)
    n_freq = ATT_HEAD_DIM // 4
    inv_freq = ROPE_THETA ** (-jnp.arange(n_freq, dtype=F32) / n_freq)
    ang_r = row[:, None] * inv_freq
    ang_c = col[:, None] * inv_freq
    cos = jnp.concatenate([jnp.cos(ang_r), jnp.cos(ang_r), jnp.cos(ang_c), jnp.cos(ang_c)], axis=-1)
    sin = jnp.concatenate([-jnp.sin(ang_r), jnp.sin(ang_r), -jnp.sin(ang_c), jnp.sin(ang_c)], axis=-1)
    return cos, sin


def _layer(x, mem, w_in, b_igate, b_fgate, att_q_norm, att_k_norm, ml_norm, w_out, ln1_g, ln1_b,
           xa_wq, xa_wk, xa_wv, xa_wo, ln2_g, ln2_b, peer_wq, peer_subkeys, peer_u, peer_v,
           ln3_g, ln3_b, *, alpha):
    B, S, D = x.shape
    T = B * S
    M = mem.shape[1]
    xt = x.reshape(T, D)

    hm = _matmul(xt, w_in[:, :MAIN_COLS].astype(BF16), bm=512, bn=512, out_dtype=BF16, name="in_proj")
    w_gate = jnp.pad(w_in[:, MAIN_COLS:], ((0, 0), (0, LANES - N_GATES))).astype(BF16)
    gates = _matmul(xt, w_gate, bm=512, bn=LANES, out_dtype=F32, name="gate_proj")
    hm3 = hm.reshape(B, S, MAIN_COLS)

    cos, sin = _rope_tables(S)
    gains = jnp.concatenate([
        jnp.broadcast_to(att_q_norm * (ATT_HEAD_DIM ** -0.5), (ATT_HEADS, ATT_HEAD_DIM)),
        jnp.broadcast_to(att_k_norm, (ATT_KV_HEADS, ATT_HEAD_DIM)),
        jnp.zeros((16 - ATT_HEADS - ATT_KV_HEADS, ATT_HEAD_DIM), F32)], axis=0)
    qk = _qk_prep(hm, cos, sin, gains, S=S)
    att = _attention(qk.reshape(B, S, -1), hm3).reshape(T, ATT_WIDTH)

    gates_t = jnp.transpose(gates[:, :N_GATES].reshape(B, S, N_GATES), (0, 2, 1))
    bias = jnp.concatenate([b_igate.reshape(-1), b_fgate.reshape(-1)]).astype(F32)
    hdir = _mlstm(hm3, gates_t, bias).reshape(2, T, ML_WIDTH)
    ml = _ml_out(hdir, hm, ml_norm.reshape(ML_HEADS, 1, ML_V_DIM))

    w_out_b = w_out.astype(BF16)
    x1 = _proj_ln([att, ml], [w_out_b[:ATT_WIDTH], w_out_b[ATT_WIDTH:]], xt, ln1_g, ln1_b,
                  alpha=alpha, name="out_proj_ln1")

    hd = D // XA_HEADS
    xq = _matmul(x1, xa_wq.astype(BF16), bm=512, bn=512, out_dtype=BF16, name="xa_q", scale=hd ** -0.5)
    memt = mem.reshape(B * M, D)
    xk = _matmul(memt, xa_wk.astype(BF16), bm=256, bn=512, out_dtype=BF16, name="xa_k")
    xv = _matmul(memt, xa_wv.astype(BF16), bm=256, bn=512, out_dtype=BF16, name="xa_v")
    xo = _xattn(xq.reshape(B, S, D), xk.reshape(B, M, D), xv.reshape(B, M, D)).reshape(T, D)
    x2 = _proj_ln([xo], [xa_wo.astype(BF16)], x1, ln2_g, ln2_b, alpha=alpha, name="xa_out_ln2")

    pq = _matmul(x2, peer_wq.astype(BF16), bm=512, bn=512, out_dtype=BF16, name="peer_q")
    e5, g5 = _peer_route(pq, _expand_subkeys(peer_subkeys.astype(BF16)))
    experts = jnp.transpose(e5, (0, 3, 4, 1, 2)).reshape(T, PEER_SLOTS)
    gates = jnp.transpose(g5, (0, 3, 4, 1, 2)).reshape(T, PEER_SLOTS)
    gates_tiled = jnp.transpose(gates.reshape(T // GATHER_TOKENS, GATHER_TOKENS, PEER_SLOTS), (0, 2, 1))
    y = _peer_gather(x2, experts, gates_tiled, _pack_expert_tables(peer_u, peer_v))
    x3 = _res_ln(x2, y, ln3_g, ln3_b, alpha=alpha, name="peer_ln3")
    return x3.reshape(B, S, D)


def kernel(x, mem, w_in, b_igate, b_fgate, att_q_norm, att_k_norm, ml_norm, w_out, ln1_g, ln1_b,
           xa_wq, xa_wk, xa_wv, xa_wo, ln2_g, ln2_b, peer_wq, peer_subkeys, peer_u, peer_v,
           ln3_g, ln3_b):
    depth = w_in.shape[0]
    alpha = (2.0 * depth) ** 0.25
    for l in range(depth):
        x = _layer(x, mem, w_in[l], b_igate[l], b_fgate[l], att_q_norm[l], att_k_norm[l], ml_norm[l],
                   w_out[l], ln1_g[l], ln1_b[l], xa_wq[l], xa_wk[l], xa_wv[l], xa_wo[l], ln2_g[l],
                   ln2_b[l], peer_wq[l], peer_subkeys[l], peer_u[l], peer_v[l], ln3_g[l], ln3_b[l],
                   alpha=alpha)
    return x
```

```python
import functools
import math

import jax
import jax.numpy as jnp
from jax import lax
from jax.experimental import pallas as pl
from jax.experimental.pallas import tpu as pltpu

F32 = jnp.float32
BF16 = jnp.bfloat16

GRID_W = 64
ATT_HEAD_DIM = 128
ATT_HEADS = 8
ATT_KV_HEADS = 2
ATT_WIDTH = ATT_HEADS * ATT_HEAD_DIM
ROPE_THETA = 10000.0
ML_HEADS = 4
ML_QK_DIM = 128
ML_V_DIM = 256
ML_WIDTH = ML_HEADS * ML_V_DIM
XA_HEADS = 4
PEER_HEADS = 8
PEER_NKEYS = 128
PEER_TOPK = 16
PEER_SLOTS = PEER_HEADS * PEER_TOPK
LN_EPS = 1e-5
RMS_EPS = 1e-6

OFF_AQ = 0
OFF_AK = OFF_AQ + ATT_WIDTH
OFF_AV = OFF_AK + ATT_KV_HEADS * ATT_HEAD_DIM
OFF_MQ = OFF_AV + ATT_KV_HEADS * ATT_HEAD_DIM
OFF_MK = OFF_MQ + ML_HEADS * ML_QK_DIM
OFF_MV = OFF_MK + ML_HEADS * ML_QK_DIM
OFF_MO = OFF_MV + ML_WIDTH
MAIN_COLS = OFF_MO + ML_WIDTH
N_GATES = 4 * ML_HEADS

LANES = 128
SUBLANES = 8
VMEM_LIMIT = 48 * 1024 * 1024

ML_CHUNK = 256
GATHER_TOKENS = 8


def _params(sem, vmem=VMEM_LIMIT):
    return pltpu.CompilerParams(dimension_semantics=sem, vmem_limit_bytes=vmem)


def _mm_kernel(a_ref, w_ref, o_ref, ab_ref, *, scale):
    @pl.when(pl.program_id(1) == 0)
    def _():
        ab_ref[...] = a_ref[...].astype(BF16)

    acc = jnp.dot(ab_ref[...], w_ref[...], preferred_element_type=F32)
    if scale != 1.0:
        acc = acc * scale
    o_ref[...] = acc.astype(o_ref.dtype)


def _matmul(a, w, *, bm, bn, out_dtype, name, scale=1.0):
    M, K = a.shape
    N = w.shape[1]
    assert M % bm == 0 and N % bn == 0
    return pl.pallas_call(
        functools.partial(_mm_kernel, scale=scale),
        out_shape=jax.ShapeDtypeStruct((M, N), out_dtype),
        grid=(M // bm, N // bn),
        in_specs=[pl.BlockSpec((bm, K), lambda i, j: (i, 0)),
                  pl.BlockSpec((K, bn), lambda i, j: (0, j))],
        out_specs=pl.BlockSpec((bm, bn), lambda i, j: (i, j)),
        scratch_shapes=[pltpu.VMEM((bm, K), BF16)],
        compiler_params=_params(("parallel", "arbitrary")),
        name=name,
    )(a, w)


def _qk_prep_kernel(h_ref, cos_ref, sin_ref, g_ref, o_ref):
    j = pl.program_id(1)
    z = h_ref[...].astype(F32)
    ms = jnp.mean(z * z, axis=-1, keepdims=True)
    zn = z * lax.rsqrt(ms + RMS_EPS) * g_ref[pl.ds(j, 1), :]
    lane = lax.broadcasted_iota(jnp.int32, zn.shape, 1)
    partner = jnp.where((lane % 64) < 32, pltpu.roll(zn, 96, 1), pltpu.roll(zn, 32, 1))
    o_ref[...] = (zn * cos_ref[...] + partner * sin_ref[...]).astype(o_ref.dtype)


def _qk_prep(hm, cos, sin, gains, *, S, tm=512):
    T = hm.shape[0]
    n_heads = ATT_HEADS + ATT_KV_HEADS
    s_blocks = S // tm
    return pl.pallas_call(
        _qk_prep_kernel,
        out_shape=jax.ShapeDtypeStruct((T, n_heads * ATT_HEAD_DIM), BF16),
        grid=(T // tm, n_heads),
        in_specs=[pl.BlockSpec((tm, LANES), lambda i, j: (i, j)),
                  pl.BlockSpec((tm, LANES), lambda i, j: (i % s_blocks, 0)),
                  pl.BlockSpec((tm, LANES), lambda i, j: (i % s_blocks, 0)),
                  pl.BlockSpec((16, LANES), lambda i, j: (0, 0))],
        out_specs=pl.BlockSpec((tm, LANES), lambda i, j: (i, j)),
        compiler_params=_params(("parallel", "arbitrary")),
        name="qk_prep",
    )(hm, cos, sin, gains)


def _attn_kernel(q_ref, k_ref, v_ref, o_ref):
    s = lax.dot_general(q_ref[...], k_ref[...], (((1,), (1,)), ((), ())),
                        preferred_element_type=F32)
    m = jnp.max(s, axis=-1, keepdims=True)
    p = jnp.exp(s - m)
    l = jnp.sum(p, axis=-1, keepdims=True)
    o = jnp.dot(p.astype(BF16), v_ref[...], preferred_element_type=F32)
    o_ref[...] = (o / l).astype(o_ref.dtype)


def _attention(qk3, hm3, *, tq=256):
    B, S, _ = qk3.shape
    group = ATT_HEADS // ATT_KV_HEADS
    k_blk0 = ATT_HEADS
    v_blk0 = OFF_AV // LANES
    return pl.pallas_call(
        _attn_kernel,
        out_shape=jax.ShapeDtypeStruct((B, S, ATT_WIDTH), BF16),
        grid=(B, ATT_HEADS, S // tq),
        in_specs=[pl.BlockSpec((None, tq, LANES), lambda b, h, i: (b, i, h)),
                  pl.BlockSpec((None, S, LANES), lambda b, h, i: (b, 0, k_blk0 + h // group)),
                  pl.BlockSpec((None, S, LANES), lambda b, h, i: (b, 0, v_blk0 + h // group))],
        out_specs=pl.BlockSpec((None, tq, LANES), lambda b, h, i: (b, i, h)),
        compiler_params=_params(("parallel", "parallel", "arbitrary")),
        name="gqa_attention",
    )(qk3, qk3, hm3)


def _log_sigmoid(x):
    return -(jnp.maximum(-x, 0.0) + jnp.log1p(jnp.exp(-jnp.abs(x))))


def _mlstm_kernel(bias_ref, q_ref, k_ref, v_ref, gi_row_ref, gf_row_ref, gi_col_ref, gf_col_ref,
                  o_ref, c_ref, n_ref, m_ref):
    h = pl.program_id(1)
    d = pl.program_id(2)
    c = pl.program_id(3)
    L = q_ref.shape[0]

    @pl.when(c == 0)
    def _():
        c_ref[...] = jnp.zeros_like(c_ref)
        n_ref[...] = jnp.zeros_like(n_ref)
        m_ref[...] = jnp.zeros_like(m_ref)

    b_i = bias_ref[d * ML_HEADS + h]
    b_f = bias_ref[2 * ML_HEADS + d * ML_HEADS + h]
    i_row = gi_row_ref[...] + b_i
    i_col = gi_col_ref[...] + b_i
    f_row = _log_sigmoid(gf_row_ref[...] + b_f)
    f_col = _log_sigmoid(gf_col_ref[...] + b_f)

    t_idx = lax.broadcasted_iota(jnp.int32, (L, L), 0)
    s_idx = lax.broadcasted_iota(jnp.int32, (L, L), 1)
    sgn = 1 - 2 * d
    within = (t_idx - s_idx) * sgn >= 0
    within_t = (s_idx - t_idx) * sgn >= 0
    b_col = jnp.sum(jnp.where(within, f_row, 0.0), axis=1, keepdims=True)
    b_row = jnp.sum(jnp.where(within_t, f_col, 0.0), axis=0, keepdims=True)
    b_tot = jnp.sum(f_row, axis=1, keepdims=True)

    m_prev = m_ref[...]
    log_d = jnp.where(within, b_col - b_row + i_row, -jnp.inf)
    log_inter = b_col + m_prev
    m_row = jnp.maximum(log_inter, jnp.max(log_d, axis=1, keepdims=True))
    dmat = jnp.exp(log_d - m_row)
    w_inter = jnp.exp(log_inter - m_row)

    q = q_ref[...]
    k = k_ref[...]
    v = v_ref[...]
    k_scale = ML_QK_DIM ** -0.5
    qk = lax.dot_general(q, k, (((1,), (1,)), ((), ())), preferred_element_type=F32)
    qk = qk * k_scale * dmat
    c_prev = c_ref[...]
    n_prev = n_ref[...]
    num = (jnp.dot(qk.astype(BF16), v, preferred_element_type=F32)
           + w_inter * jnp.dot(q, c_prev.astype(BF16), preferred_element_type=F32))
    den = (jnp.sum(qk, axis=1, keepdims=True)
           + w_inter * jnp.sum(q.astype(F32) * n_prev, axis=1, keepdims=True))
    o_ref[...] = (num / jnp.maximum(jnp.abs(den), jnp.exp(-m_row))).astype(o_ref.dtype)

    log_w = b_tot - b_col + i_col
    m_new = jnp.maximum(b_tot + m_prev, jnp.max(log_w, axis=0, keepdims=True))
    w = jnp.exp(log_w - m_new) * k_scale
    decay = jnp.exp(b_tot + m_prev - m_new)
    kw = k.astype(F32) * w
    c_ref[...] = decay * c_prev + lax.dot_general(
        kw.astype(BF16), v, (((0,), (0,)), ((), ())), preferred_element_type=F32)
    n_ref[...] = decay * n_prev + jnp.sum(kw, axis=0, keepdims=True)
    m_ref[...] = m_new


def _mlstm(hm3, gates_t, bias, *, L=ML_CHUNK):
    B, S, _ = hm3.shape
    nc = S // L
    g_row = gates_t.reshape(B, N_GATES, 1, S)
    g_col = gates_t.reshape(B, N_GATES, S, 1)

    def cidx(d, c):
        return c + d * (nc - 1 - 2 * c)

    q0, k0, v0 = OFF_MQ // ML_QK_DIM, OFF_MK // ML_QK_DIM, OFF_MV // ML_V_DIM
    return pl.pallas_call(
        _mlstm_kernel,
        out_shape=jax.ShapeDtypeStruct((2, B, S, ML_WIDTH), F32),
        grid=(B, ML_HEADS, 2, nc),
        in_specs=[
            pl.BlockSpec(memory_space=pltpu.SMEM),
            pl.BlockSpec((None, L, ML_QK_DIM), lambda b, h, d, c: (b, cidx(d, c), q0 + h)),
            pl.BlockSpec((None, L, ML_QK_DIM), lambda b, h, d, c: (b, cidx(d, c), k0 + h)),
            pl.BlockSpec((None, L, ML_V_DIM), lambda b, h, d, c: (b, cidx(d, c), v0 + h)),
            pl.BlockSpec((None, None, 1, L), lambda b, h, d, c: (b, d * ML_HEADS + h, 0, cidx(d, c))),
            pl.BlockSpec((None, None, 1, L),
                         lambda b, h, d, c: (b, 2 * ML_HEADS + d * ML_HEADS + h, 0, cidx(d, c))),
            pl.BlockSpec((None, None, L, 1), lambda b, h, d, c: (b, d * ML_HEADS + h, cidx(d, c), 0)),
            pl.BlockSpec((None, None, L, 1),
                         lambda b, h, d, c: (b, 2 * ML_HEADS + d * ML_HEADS + h, cidx(d, c), 0)),
        ],
        out_specs=pl.BlockSpec((None, None, L, ML_V_DIM), lambda b, h, d, c: (d, b, cidx(d, c), h)),
        scratch_shapes=[pltpu.VMEM((ML_QK_DIM, ML_V_DIM), F32),
                        pltpu.VMEM((1, ML_QK_DIM), F32),
                        pltpu.VMEM((1, 1), F32)],
        compiler_params=_params(("parallel", "parallel", "parallel", "arbitrary")),
        name="mlstm",
    )(bias, hm3, hm3, hm3, g_row, g_row, g_col, g_col)


def _ml_out_kernel(hf_ref, hb_ref, mo_ref, g_ref, o_ref):
    hsum = hf_ref[...] + hb_ref[...]
    ms = jnp.mean(hsum * hsum, axis=-1, keepdims=True)
    hn = hsum * lax.rsqrt(ms + RMS_EPS) * g_ref[...]
    o_ref[...] = (hn * jax.nn.sigmoid(mo_ref[...].astype(F32))).astype(o_ref.dtype)


def _ml_out(hdir, hm, ml_gain, *, tm=512):
    T = hm.shape[0]
    mo0 = OFF_MO // ML_V_DIM
    return pl.pallas_call(
        _ml_out_kernel,
        out_shape=jax.ShapeDtypeStruct((T, ML_WIDTH), BF16),
        grid=(T // tm, ML_HEADS),
        in_specs=[pl.BlockSpec((None, tm, ML_V_DIM), lambda i, h: (0, i, h)),
                  pl.BlockSpec((None, tm, ML_V_DIM), lambda i, h: (1, i, h)),
                  pl.BlockSpec((tm, ML_V_DIM), lambda i, h: (i, mo0 + h)),
                  pl.BlockSpec((None, 1, ML_V_DIM), lambda i, h: (h, 0, 0))],
        out_specs=pl.BlockSpec((tm, ML_V_DIM), lambda i, h: (i, h)),
        compiler_params=_params(("parallel", "arbitrary")),
        name="mlstm_out",
    )(hdir, hdir, hm, ml_gain)


def _layer_norm_rows(z, g, b):
    zc = z - jnp.mean(z, axis=-1, keepdims=True)
    var = jnp.mean(zc * zc, axis=-1, keepdims=True)
    return zc * lax.rsqrt(var + LN_EPS) * g + b


def _proj_ln_kernel(*refs, n_in, alpha):
    a_refs = refs[:n_in]
    w_refs = refs[n_in:2 * n_in]
    x_ref, g_ref, b_ref, o_ref = refs[2 * n_in:]
    acc = alpha * x_ref[...]
    for a_ref, w_ref in zip(a_refs, w_refs):
        acc = acc + jnp.dot(a_ref[...], w_ref[...], preferred_element_type=F32)
    o_ref[...] = _layer_norm_rows(acc, g_ref[...], b_ref[...])


def _proj_ln(a_list, w_list, x, g, b, *, alpha, name, bm=256):
    T, D = x.shape
    n_in = len(a_list)
    in_specs = ([pl.BlockSpec((bm, a.shape[1]), lambda i: (i, 0)) for a in a_list]
                + [pl.BlockSpec(w.shape, lambda i: (0, 0)) for w in w_list]
                + [pl.BlockSpec((bm, D), lambda i: (i, 0)),
                   pl.BlockSpec((1, D), lambda i: (0, 0)),
                   pl.BlockSpec((1, D), lambda i: (0, 0))])
    return pl.pallas_call(
        functools.partial(_proj_ln_kernel, n_in=n_in, alpha=alpha),
        out_shape=jax.ShapeDtypeStruct((T, D), F32),
        grid=(T // bm,),
        in_specs=in_specs,
        out_specs=pl.BlockSpec((bm, D), lambda i: (i, 0)),
        compiler_params=_params(("parallel",)),
        name=name,
    )(*a_list, *w_list, x, g.reshape(1, D), b.reshape(1, D))


def _res_ln_kernel(x_ref, y_ref, g_ref, b_ref, o_ref, *, alpha):
    o_ref[...] = _layer_norm_rows(alpha * x_ref[...] + y_ref[...], g_ref[...], b_ref[...])


def _res_ln(x, y, g, b, *, alpha, name, bm=512):
    T, D = x.shape
    return pl.pallas_call(
        functools.partial(_res_ln_kernel, alpha=alpha),
        out_shape=jax.ShapeDtypeStruct((T, D), F32),
        grid=(T // bm,),
        in_specs=[pl.BlockSpec((bm, D), lambda i: (i, 0)),
                  pl.BlockSpec((bm, D), lambda i: (i, 0)),
                  pl.BlockSpec((1, D), lambda i: (0, 0)),
                  pl.BlockSpec((1, D), lambda i: (0, 0))],
        out_specs=pl.BlockSpec((bm, D), lambda i: (i, 0)),
        compiler_params=_params(("parallel",)),
        name=name,
    )(x, y, g.reshape(1, D), b.reshape(1, D))


def _xattn_kernel(q_ref, k_ref, v_ref, o_ref):
    hd = q_ref.shape[1] // XA_HEADS
    for h in range(XA_HEADS):
        sl = slice(h * hd, (h + 1) * hd)
        s = lax.dot_general(q_ref[:, sl], k_ref[:, sl], (((1,), (1,)), ((), ())),
                            preferred_element_type=F32)
        m = jnp.max(s, axis=-1, keepdims=True)
        p = jnp.exp(s - m)
        l = jnp.sum(p, axis=-1, keepdims=True)
        o = jnp.dot(p.astype(BF16), v_ref[:, sl], preferred_element_type=F32)
        o_ref[:, sl] = (o / l).astype(o_ref.dtype)


def _xattn(q3, k3, v3, *, tq=512):
    B, S, D = q3.shape
    M = k3.shape[1]
    return pl.pallas_call(
        _xattn_kernel,
        out_shape=jax.ShapeDtypeStruct((B, S, D), BF16),
        grid=(B, S // tq),
        in_specs=[pl.BlockSpec((None, tq, D), lambda b, i: (b, i, 0)),
                  pl.BlockSpec((None, M, D), lambda b, i: (b, 0, 0)),
                  pl.BlockSpec((None, M, D), lambda b, i: (b, 0, 0))],
        out_specs=pl.BlockSpec((None, tq, D), lambda b, i: (b, i, 0)),
        compiler_params=_params(("parallel", "arbitrary")),
        name="mem_xattn",
    )(q3, k3, v3)


def _oddeven_merge_sort(n):
    pairs = []

    def merge(lo, hi, r):
        step = r * 2
        if step < hi - lo:
            merge(lo, hi, step)
            merge(lo + r, hi, step)
            for i in range(lo + r, hi - r, step):
                pairs.append((i, i + r))
        else:
            pairs.append((lo, lo + r))

    def sort(lo, hi):
        if hi - lo >= 1:
            mid = lo + (hi - lo) // 2
            sort(lo, mid)
            sort(mid + 1, hi)
            merge(lo, hi, 1)

    sort(0, n - 1)
    return pairs


def _bitonic_merge(n):
    pairs = []
    k = n // 2
    while k >= 1:
        pairs.extend((i, i | k) for i in range(n) if i & k == 0)
        k //= 2
    return pairs


_SORT_TOPK = _oddeven_merge_sort(PEER_TOPK)
_BITONIC_TOPK = _bitonic_merge(PEER_TOPK)


def _ranks_before(a, b):
    return jnp.where(a[0] == b[0], a[1] - b[1], b[0] - a[0]) > 0


def _first(a, b):
    swap = _ranks_before(a, b)
    return (jnp.maximum(a[0], b[0]),) + tuple(jnp.where(swap, fb, fa) for fa, fb in zip(a[1:], b[1:]))


def _exchange(a, b):
    swap = _ranks_before(a, b)
    first = (jnp.maximum(a[0], b[0]),) + tuple(jnp.where(swap, fb, fa) for fa, fb in zip(a[1:], b[1:]))
    second = (jnp.minimum(a[0], b[0]),) + tuple(jnp.where(swap, fa, fb) for fa, fb in zip(a[1:], b[1:]))
    return first, second


def _run_network(network, elements):
    elements = list(elements)
    for i, j in network:
        elements[i], elements[j] = _exchange(elements[i], elements[j])
    return elements


def _merge_top(a, b, *, sort):
    k = PEER_TOPK
    c = [(_first(a[i], b[k - 1 - i]) if k - 1 - i < len(b) else a[i]) for i in range(k)]
    return _run_network(_BITONIC_TOPK, c) if sort else c


def _top_keys(scores):
    k = PEER_TOPK
    groups = [_run_network(_SORT_TOPK, [(scores[g * k + r], float(g * k + r)) for r in range(k)])
              for g in range(len(scores) // k)]
    while len(groups) > 1:
        groups = [_merge_top(groups[i], groups[i + 1], sort=True) for i in range(0, len(groups), 2)]
    return groups[0]


def _top_pairs(s0, s1):
    k = PEER_TOPK
    assert k == 16

    def el(a, b):
        return (s0[a][0] + s1[b][0], float(a * k + b), s0[a][1] * float(PEER_NKEYS) + s1[b][1])

    def row(a):
        return [el(a, b) for b in range(k // (a + 1))]

    col_tail = [el(a, 0) for a in range(k // 2, k)]
    g2 = _run_network(_BITONIC_TOPK, row(1) + col_tail[::-1])
    g3 = _run_network(_SORT_TOPK, row(2) + row(3) + row(4) + row(5) + row(6))
    t1 = _merge_top(row(0), g2, sort=True)
    t2 = _merge_top(g3, row(7), sort=True)
    return _merge_top(t1, t2, sort=False)


def _peer_route_kernel(q_ref, kexp_ref, e_ref, g_ref):
    tops = []
    for c in range(2):
        q_t = jnp.concatenate([q_ref[a * LANES:(a + 1) * LANES, c * PEER_NKEYS:(c + 1) * PEER_NKEYS]
                               for a in range(SUBLANES)], axis=1)
        s = lax.dot_general(kexp_ref[c], q_t, (((1,), (1,)), ((), ())), preferred_element_type=F32)
        tops.append(_top_keys([s[k * SUBLANES:(k + 1) * SUBLANES, :] for k in range(PEER_NKEYS)]))
    best = _top_pairs(tops[0], tops[1])
    m = functools.reduce(jnp.maximum, [b[0] for b in best])
    p = [jnp.exp(b[0] - m) for b in best]
    inv = 1.0 / functools.reduce(lambda x, y: x + y, p)
    for r in range(PEER_TOPK):
        g_ref[r] = p[r] * inv
        e_ref[r] = best[r][2].astype(jnp.int32)


def _expand_subkeys(subkeys):
    h, c, k, d = subkeys.shape
    eye = jnp.eye(SUBLANES, dtype=subkeys.dtype)
    return jnp.einsum('hckd,ab->hckabd', subkeys, eye).reshape(h, c, k * SUBLANES, SUBLANES * d)


def _peer_route(q, kexp):
    T = q.shape[0]
    tile = SUBLANES * LANES
    heads = kexp.shape[0]
    shape = (T // tile, heads, PEER_TOPK, SUBLANES, LANES)
    out_spec = pl.BlockSpec((None, None, PEER_TOPK, SUBLANES, LANES), lambda h, i: (i, h, 0, 0, 0))
    return pl.pallas_call(
        _peer_route_kernel,
        out_shape=(jax.ShapeDtypeStruct(shape, jnp.int32), jax.ShapeDtypeStruct(shape, F32)),
        grid=(heads, T // tile),
        in_specs=[pl.BlockSpec((tile, 2 * PEER_NKEYS), lambda h, i: (i, h)),
                  pl.BlockSpec((None,) + kexp.shape[1:], lambda h, i: (h, 0, 0, 0))],
        out_specs=(out_spec, out_spec),
        compiler_params=_params(("parallel", "arbitrary")),
        name="peer_route",
    )(q, kexp)


def _gelu(x):
    return 0.5 * x * (1.0 + lax.erf(x * (1.0 / math.sqrt(2.0))))


def _pack_expert_tables(u, v):
    half = u.shape[1] // 2

    def pack(a):
        b = lax.bitcast_convert_type(a.astype(BF16), jnp.uint16).astype(jnp.uint32)
        return b[:, :half] | (b[:, half:] << 16)

    return jnp.concatenate([pack(u), pack(v)], axis=1).reshape(u.shape[0], 1, 2 * half)


def _unpack_bf16_pair(w):
    lo = lax.bitcast_convert_type(w << 16, F32)
    hi = lax.bitcast_convert_type(w & jnp.uint32(0xFFFF0000), F32)
    return lo, hi


def _issue_rows(e_ref, t, table_hbm, dst, sem, lo, hi):
    for j in range(lo, hi):
        pltpu.async_copy(table_hbm.at[e_ref[t, j]], dst.at[pl.ds(t * PEER_SLOTS + j, 1)], sem,
                         priority=j % 2)


def _wait_rows(table_hbm, dst, sem):
    pltpu.make_async_copy(table_hbm.at[pl.ds(0, dst.shape[0]), 0], dst, sem).wait()


def _peer_gather_step(e_next_ref, x_ref, g_ref, table_hbm, o_ref, cur, nxt, sem_cur, sem_nxt):
    tt, d = x_ref.shape
    half = d // 2
    n_chunks = half // LANES
    first = PEER_SLOTS // 2
    _wait_rows(table_hbm, cur, sem_cur)
    lane = lax.broadcasted_iota(jnp.int32, (PEER_SLOTS, tt), 1)
    hmat = jnp.zeros((PEER_SLOTS, tt), F32)
    for t in range(tt):
        _issue_rows(e_next_ref, t, table_hbm, nxt, sem_nxt, 0, first)
        r0 = t * PEER_SLOTS
        acc = None
        for c in range(n_chunks):
            lo, hi = _unpack_bf16_pair(cur[r0:r0 + PEER_SLOTS, c * LANES:(c + 1) * LANES])
            term = (lo * x_ref[t:t + 1, c * LANES:(c + 1) * LANES]
                    + hi * x_ref[t:t + 1, half + c * LANES:half + (c + 1) * LANES])
            acc = term if acc is None else acc + term
        hmat = jnp.where(lane == t, jnp.sum(acc, axis=1, keepdims=True), hmat)
    a = _gelu(hmat) * g_ref[...]
    for t in range(tt):
        _issue_rows(e_next_ref, t, table_hbm, nxt, sem_nxt, first, PEER_SLOTS)
        r0 = t * PEER_SLOTS
        a_t = a[:, t:t + 1]
        for c in range(n_chunks):
            lo, hi = _unpack_bf16_pair(cur[r0:r0 + PEER_SLOTS, half + c * LANES:half + (c + 1) * LANES])
            o_ref[t:t + 1, c * LANES:(c + 1) * LANES] = jnp.sum(lo * a_t, axis=0, keepdims=True)
            o_ref[t:t + 1, half + c * LANES:half + (c + 1) * LANES] = jnp.sum(hi * a_t, axis=0, keepdims=True)


def _peer_gather_kernel(e_first_ref, e_second_ref, e_next_ref, x_ref, g_ref, table_hbm, o_ref,
                        buf0, buf1, buf2, sem, *, n_steps):
    i = pl.program_id(0)
    tt = x_ref.shape[0]
    bufs = (buf0, buf1, buf2)

    @pl.when(i == 0)
    def _():
        for t in range(tt):
            _issue_rows(e_first_ref, t, table_hbm, buf0, sem.at[0], 0, PEER_SLOTS)
        for t in range(tt):
            _issue_rows(e_second_ref, t, table_hbm, buf1, sem.at[1], 0, PEER_SLOTS)

    for k in range(3):
        @pl.when(i % 3 == k)
        def _(k=k):
            _peer_gather_step(e_next_ref, x_ref, g_ref, table_hbm, o_ref, bufs[k], bufs[(k + 2) % 3],
                              sem.at[k], sem.at[(k + 2) % 3])

    @pl.when(i == n_steps - 1)
    def _():
        for k in range(3):
            @pl.when(i % 3 == k)
            def _(k=k):
                _wait_rows(table_hbm, bufs[(k + 1) % 3], sem.at[(k + 1) % 3])
                _wait_rows(table_hbm, bufs[(k + 2) % 3], sem.at[(k + 2) % 3])


def _peer_gather(x, experts, gates_tiled, table, *, tt=GATHER_TOKENS):
    T, D = x.shape
    n_steps = T // tt
    return pl.pallas_call(
        functools.partial(_peer_gather_kernel, n_steps=n_steps),
        out_shape=jax.ShapeDtypeStruct((T, D), F32),
        grid=(n_steps,),
        in_specs=[pl.BlockSpec((tt, PEER_SLOTS), lambda i: (i, 0), memory_space=pltpu.SMEM),
                  pl.BlockSpec((tt, PEER_SLOTS), lambda i: (jnp.minimum(i + 1, n_steps - 1), 0),
                               memory_space=pltpu.SMEM),
                  pl.BlockSpec((tt, PEER_SLOTS), lambda i: (jnp.minimum(i + 2, n_steps - 1), 0),
                               memory_space=pltpu.SMEM),
                  pl.BlockSpec((tt, D), lambda i: (i, 0)),
                  pl.BlockSpec((None, PEER_SLOTS, tt), lambda i: (i, 0, 0)),
                  pl.BlockSpec(memory_space=pl.ANY)],
        out_specs=pl.BlockSpec((tt, D), lambda i: (i, 0)),
        scratch_shapes=[pltpu.VMEM((tt * PEER_SLOTS, D), jnp.uint32),
                        pltpu.VMEM((tt * PEER_SLOTS, D), jnp.uint32),
                        pltpu.VMEM((tt * PEER_SLOTS, D), jnp.uint32),
                        pltpu.SemaphoreType.DMA((3,))],
        compiler_params=_params(("arbitrary",)),
        name="peer_gather",
    )(experts, experts, experts, x, gates_tiled, table)


def _rope_tables(S):
    pos = jnp.arange(S, dtype=jnp.int32)
    row = (pos // GRID_W).astype(F32)
    col = (pos % GRID_W).astype(F32)
    n_freq = ATT_HEAD_DIM // 4
    inv_freq = ROPE_THETA ** (-jnp.arange(n_freq, dtype=F32) / n_freq)
    ang_r = row[:, None] * inv_freq
    ang_c = col[:, None] * inv_freq
    cos = jnp.concatenate([jnp.cos(ang_r), jnp.cos(ang_r), jnp.cos(ang_c), jnp.cos(ang_c)], axis=-1)
    sin = jnp.concatenate([-jnp.sin(ang_r), jnp.sin(ang_r), -jnp.sin(ang_c), jnp.sin(ang_c)], axis=-1)
    return cos, sin


def _layer(x, mem, w_in, b_igate, b_fgate, att_q_norm, att_k_norm, ml_norm, w_out, ln1_g, ln1_b,
           xa_wq, xa_wk, xa_wv, xa_wo, ln2_g, ln2_b, peer_wq, peer_subkeys, peer_u, peer_v,
           ln3_g, ln3_b, *, alpha):
    B, S, D = x.shape
    T = B * S
    M = mem.shape[1]
    xt = x.reshape(T, D)

    hm = _matmul(xt, w_in[:, :MAIN_COLS].astype(BF16), bm=512, bn=512, out_dtype=BF16, name="in_proj")
    w_gate = jnp.pad(w_in[:, MAIN_COLS:], ((0, 0), (0, LANES - N_GATES))).astype(BF16)
    gates = _matmul(xt, w_gate, bm=512, bn=LANES, out_dtype=F32, name="gate_proj")
    hm3 = hm.reshape(B, S, MAIN_COLS)

    cos, sin = _rope_tables(S)
    gains = jnp.concatenate([
        jnp.broadcast_to(att_q_norm * (ATT_HEAD_DIM ** -0.5), (ATT_HEADS, ATT_HEAD_DIM)),
        jnp.broadcast_to(att_k_norm, (ATT_KV_HEADS, ATT_HEAD_DIM)),
        jnp.zeros((16 - ATT_HEADS - ATT_KV_HEADS, ATT_HEAD_DIM), F32)], axis=0)
    qk = _qk_prep(hm, cos, sin, gains, S=S)
    att = _attention(qk.reshape(B, S, -1), hm3).reshape(T, ATT_WIDTH)

    gates_t = jnp.transpose(gates[:, :N_GATES].reshape(B, S, N_GATES), (0, 2, 1))
    bias = jnp.concatenate([b_igate.reshape(-1), b_fgate.reshape(-1)]).astype(F32)
    hdir = _mlstm(hm3, gates_t, bias).reshape(2, T, ML_WIDTH)
    ml = _ml_out(hdir, hm, ml_norm.reshape(ML_HEADS, 1, ML_V_DIM))

    w_out_b = w_out.astype(BF16)
    x1 = _proj_ln([att, ml], [w_out_b[:ATT_WIDTH], w_out_b[ATT_WIDTH:]], xt, ln1_g, ln1_b,
                  alpha=alpha, name="out_proj_ln1")

    hd = D // XA_HEADS
    xq = _matmul(x1, xa_wq.astype(BF16), bm=512, bn=512, out_dtype=BF16, name="xa_q", scale=hd ** -0.5)
    memt = mem.reshape(B * M, D)
    xk = _matmul(memt, xa_wk.astype(BF16), bm=256, bn=512, out_dtype=BF16, name="xa_k")
    xv = _matmul(memt, xa_wv.astype(BF16), bm=256, bn=512, out_dtype=BF16, name="xa_v")
    xo = _xattn(xq.reshape(B, S, D), xk.reshape(B, M, D), xv.reshape(B, M, D)).reshape(T, D)
    x2 = _proj_ln([xo], [xa_wo.astype(BF16)], x1, ln2_g, ln2_b, alpha=alpha, name="xa_out_ln2")

    pq = _matmul(x2, peer_wq.astype(BF16), bm=512, bn=512, out_dtype=BF16, name="peer_q")
    e5, g5 = _peer_route(pq, _expand_subkeys(peer_subkeys.astype(BF16)))
    experts = jnp.transpose(e5, (0, 3, 4, 1, 2)).reshape(T, PEER_SLOTS)
    gates = jnp.transpose(g5, (0, 3, 4, 1, 2)).reshape(T, PEER_SLOTS)
    gates_tiled = jnp.transpose(gates.reshape(T // GATHER_TOKENS, GATHER_TOKENS, PEER_SLOTS), (0, 2, 1))
    y = _peer_gather(x2, experts, gates_tiled, _pack_expert_tables(peer_u, peer_v))
    x3 = _res_ln(x2, y, ln3_g, ln3_b, alpha=alpha, name="peer_ln3")
    return x3.reshape(B, S, D)


def kernel(x, mem, w_in, b_igate, b_fgate, att_q_norm, att_k_norm, ml_norm, w_out, ln1_g, ln1_b,
           xa_wq, xa_wk, xa_wv, xa_wo, ln2_g, ln2_b, peer_wq, peer_subkeys, peer_u, peer_v,
           ln3_g, ln3_b):
    depth = w_in.shape[0]
    alpha = (2.0 * depth) ** 0.25
    for l in range(depth):
        x = _layer(x, mem, w_in[l], b_igate[l], b_fgate[l], att_q_norm[l], att_k_norm[l], ml_norm[l],
                   w_out[l], ln1_g[l], ln1_b[l], xa_wq[l], xa_wk[l], xa_wv[l], xa_wo[l], ln2_g[l],
                   ln2_b[l], peer_wq[l], peer_subkeys[l], peer_u[l], peer_v[l], ln3_g[l], ln3_b[l],
                   alpha=alpha)
    return x
```
